```python
import math
import jax, jax.numpy as jnp
from jax import lax
import numpy as np

D_MODEL = 2048
BATCH = 4
SEQ = 2048
DEPTH = 2

CHUNK = 64
N_META = 16
Q_BLOCK = 128
HEAD_DIM = 64
EPS = 1e-6
NEG = -1e30

A_HEADS = 16
A_KV_HEADS = 4
A_GROUP = A_HEADS // A_KV_HEADS
WINDOW = 128
WINDOW_CHUNKS = WINDOW // CHUNK
BAND_BACK = WINDOW + CHUNK
BAND_LEN = BAND_BACK + Q_BLOCK + CHUNK

B_HEADS = 16

C_HEADS = D_MODEL // (2 * HEAD_DIM)

D_FF = 4 * D_MODEL

N_EVEN = (DEPTH + 1) // 2
N_ODD = DEPTH // 2

A_Q = A_HEADS * HEAD_DIM
A_KV = A_KV_HEADS * HEAD_DIM
B_W = B_HEADS * HEAD_DIM
AB_IN = A_Q + 2 * A_KV + 3 * B_W
AB_OUT = A_Q + B_W
AB_SPLITS = [A_Q, A_Q + A_KV, A_Q + 2 * A_KV, A_Q + 2 * A_KV + B_W, A_Q + 2 * A_KV + 2 * B_W]
C_QK = C_HEADS * 2 * HEAD_DIM
C_IN = 3 * C_QK
C_OUT = C_HEADS * 2 * HEAD_DIM

kernel_name = "chunk_causal_hybrid_swa_sink_stickbreak_diffattn"


def rmsnorm(x, g):
    x32 = x.astype(jnp.float32)
    y = x32 * lax.rsqrt(jnp.mean(x32 * x32, axis=-1, keepdims=True) + EPS)
    return (y * g.astype(jnp.float32)).astype(x.dtype)


def chunk_id(pos):
    return jnp.where(pos < N_META, 0, 1 + (pos - N_META) // CHUNK)


def chunk_end(p):
    if p < N_META:
        return N_META
    return N_META + ((p - N_META) // CHUNK + 1) * CHUNK


def alibi_slopes(n):
    return jnp.exp2(-8.0 * (jnp.arange(n, dtype=jnp.float32) + 1.0) / n)


def sliding_sink_attention(q, k, v, sinks):
    bsz, lp = q.shape[0], q.shape[1]
    nb = lp // Q_BLOCK
    scale = HEAD_DIM ** -0.5
    qb = q.reshape(bsz, nb, Q_BLOCK, A_KV_HEADS, A_GROUP, HEAD_DIM)
    q_pos = jnp.arange(lp).reshape(nb, Q_BLOCK)
    k_pos = q_pos[:, :1] - BAND_BACK + jnp.arange(BAND_LEN)[None, :]
    k_idx = jnp.clip(k_pos, 0, lp - 1)
    kb = k[:, k_idx]
    vb = v[:, k_idx]
    km, vm = k[:, :N_META], v[:, :N_META]

    s_meta = jnp.einsum('bnqhgd,bmhd->bnhgqm', qb, km).astype(jnp.float32)
    s_band = jnp.einsum('bnqhgd,bnkhd->bnhgqk', qb, kb).astype(jnp.float32)
    s = jnp.concatenate([s_meta, s_band], axis=-1) * scale

    meta_pos = jnp.broadcast_to(jnp.arange(N_META)[None, :], (nb, N_META))
    kpos_all = jnp.concatenate([meta_pos, k_pos], axis=-1)
    qc = chunk_id(q_pos)[:, :, None]
    kc = chunk_id(k_pos)[:, None, :]
    band_ok = (k_pos[:, None, :] >= N_META) & (k_pos[:, None, :] < lp) & (kc <= qc) & (kc >= qc - WINDOW_CHUNKS)
    mask = jnp.concatenate([jnp.ones((nb, Q_BLOCK, N_META), bool), band_ok], axis=-1)
    dist = jnp.abs(q_pos[:, :, None] - kpos_all[:, None, :]).astype(jnp.float32)
    slopes = alibi_slopes(A_HEADS).reshape(A_KV_HEADS, A_GROUP)
    bias = -slopes[None, :, :, None, None] * dist[:, None, None]
    s = jnp.where(mask[None, :, None, None], s + bias[None], NEG)

    sink = sinks.astype(jnp.float32).reshape(A_KV_HEADS, A_GROUP)[None, None, :, :, None, None]
    sink = jnp.broadcast_to(sink, s.shape[:-1] + (1,))
    p = jax.nn.softmax(jnp.concatenate([s, sink], axis=-1), axis=-1)[..., :-1].astype(v.dtype)
    o = (jnp.einsum('bnhgqm,bmhd->bnqhgd', p[..., :N_META], vm)
         + jnp.einsum('bnhgqk,bnkhd->bnqhgd', p[..., N_META:], vb))
    return o.reshape(bsz, lp, A_HEADS * HEAD_DIM)


def stick_breaking_attention(q, k, v):
    lp = q.shape[1]
    scale = HEAD_DIM ** -0.5
    outs = []
    for q0 in range(0, lp, Q_BLOCK):
        q1 = q0 + Q_BLOCK
        z = jnp.einsum('bqhd,bkhd->bhqk', q[:, q0:q1], k[:, :q1]).astype(jnp.float32) * scale
        t_pos = jnp.arange(q0, q1)[:, None]
        s_pos = jnp.arange(q1)[None, :]
        strict = s_pos < t_pos
        log_keep = jnp.where(strict, jax.nn.log_sigmoid(-z), 0.0)
        between = lax.cumsum(log_keep, axis=3, reverse=True) - log_keep
        w = jnp.where(strict, jnp.exp(jax.nn.log_sigmoid(z) + between), 0.0)
        outs.append(jnp.einsum('bhqk,bkhd->bqhd', w.astype(v.dtype), v[:, :q1]))
    o = jnp.concatenate(outs, axis=1)
    return o.reshape(o.shape[0], lp, B_HEADS * HEAD_DIM)


def differential_attention(q, k, v, lam_vecs, subln_g, lambda_init):
    bsz, lp = q.shape[0], q.shape[1]
    scale = HEAD_DIM ** -0.5
    lv = lam_vecs.astype(jnp.float32)
    lam = jnp.exp(jnp.sum(lv[0] * lv[1])) - jnp.exp(jnp.sum(lv[2] * lv[3])) + lambda_init
    slopes = alibi_slopes(C_HEADS)
    outs = []
    for q0 in range(0, lp, Q_BLOCK):
        q1 = q0 + Q_BLOCK
        kend = min(lp, chunk_end(q1 - 1))
        s = jnp.einsum('bqhcd,bkhcd->bhcqk', q[:, q0:q1], k[:, :kend]).astype(jnp.float32) * scale
        qp = jnp.arange(q0, q1)
        kp = jnp.arange(kend)
        mask = chunk_id(kp)[None, :] <= chunk_id(qp)[:, None]
        dist = jnp.abs(qp[:, None] - kp[None, :]).astype(jnp.float32)
        bias = -slopes[:, None, None, None] * dist[None, None]
        s = jnp.where(mask, s + bias, NEG)
        p = jax.nn.softmax(s, axis=-1)
        w = (p[:, :, 0] - lam * p[:, :, 1]).astype(v.dtype)
        outs.append(jnp.einsum('bhqk,bkhe->bqhe', w, v[:, :kend]))
    o = jnp.concatenate(outs, axis=1)
    o = rmsnorm(o, subln_g) * (1.0 - lambda_init)
    return o.reshape(bsz, lp, C_HEADS * 2 * HEAD_DIM)


def setup_inputs(seed: int = 0) -> dict:
    key = jax.random.key(seed)
    ks = jax.random.split(key, 16)
    f32 = jnp.float32

    def dense(k, shape, fan_in):
        return jax.random.normal(k, shape, f32) * fan_in ** -0.5

    def gain(k, shape):
        return 1.0 + 0.02 * jax.random.normal(k, shape, f32)

    return {
        "x": jax.random.normal(ks[0], (BATCH, SEQ, D_MODEL), f32),
        "meta_tokens": jax.random.normal(ks[1], (N_META, D_MODEL), f32),
        "ab_norm": gain(ks[2], (N_EVEN, D_MODEL)),
        "w_in_ab": dense(ks[3], (N_EVEN, D_MODEL, AB_IN), D_MODEL),
        "attn_sinks": jax.random.normal(ks[4], (N_EVEN, A_HEADS), f32),
        "w_out_ab": dense(ks[5], (N_EVEN, AB_OUT, D_MODEL), AB_OUT),
        "c_norm": gain(ks[6], (N_ODD, D_MODEL)),
        "w_in_c": dense(ks[7], (N_ODD, D_MODEL, C_IN), D_MODEL),
        "diff_lambda": 0.1 * jax.random.normal(ks[8], (N_ODD, 4, HEAD_DIM), f32),
        "diff_subln": gain(ks[9], (N_ODD, 2 * HEAD_DIM)),
        "w_out_c": dense(ks[10], (N_ODD, C_OUT, D_MODEL), C_OUT),
        "mlp_norm": gain(ks[11], (DEPTH, D_MODEL)),
        "w_mlp_in": dense(ks[12], (DEPTH, D_MODEL, D_FF), D_MODEL),
        "w_mlp_out": dense(ks[13], (DEPTH, D_FF, D_MODEL), D_FF),
        "final_norm": gain(ks[14], (D_MODEL,)),
    }


def reference(x, meta_tokens, ab_norm, w_in_ab, attn_sinks, w_out_ab, c_norm, w_in_c,
              diff_lambda, diff_subln, w_out_c, mlp_norm, w_mlp_in, w_mlp_out, final_norm):
    bsz, seq = x.shape[0], x.shape[1]
    total = N_META + seq
    lp = ((total + Q_BLOCK - 1) // Q_BLOCK) * Q_BLOCK
    meta = jnp.broadcast_to(meta_tokens.astype(x.dtype)[None], (bsz, N_META, D_MODEL))
    h = jnp.concatenate([meta, x], axis=1)
    h = jnp.pad(h, ((0, 0), (0, lp - total), (0, 0)))

    for layer in range(DEPTH):
        if layer % 2 == 0:
            i = layer // 2
            hn = rmsnorm(h, ab_norm[i])
            proj = hn @ w_in_ab[i]
            qa, ka, va, qb, kb, vb = jnp.split(proj, AB_SPLITS, axis=-1)
            out_a = sliding_sink_attention(
                qa.reshape(bsz, lp, A_HEADS, HEAD_DIM),
                ka.reshape(bsz, lp, A_KV_HEADS, HEAD_DIM),
                va.reshape(bsz, lp, A_KV_HEADS, HEAD_DIM),
                attn_sinks[i])
            out_b = stick_breaking_attention(
                qb.reshape(bsz, lp, B_HEADS, HEAD_DIM),
                kb.reshape(bsz, lp, B_HEADS, HEAD_DIM),
                vb.reshape(bsz, lp, B_HEADS, HEAD_DIM))
            h = h + jnp.concatenate([out_a, out_b], axis=-1) @ w_out_ab[i]
        else:
            i = layer // 2
            lambda_init = 0.8 - 0.6 * math.exp(-0.3 * layer)
            hn = rmsnorm(h, c_norm[i])
            proj = hn @ w_in_c[i]
            qc, kc, vc = jnp.split(proj, 3, axis=-1)
            out_c = differential_attention(
                qc.reshape(bsz, lp, C_HEADS, 2, HEAD_DIM),
                kc.reshape(bsz, lp, C_HEADS, 2, HEAD_DIM),
                vc.reshape(bsz, lp, C_HEADS, 2 * HEAD_DIM),
                diff_lambda[i], diff_subln[i], lambda_init)
            h = h + out_c @ w_out_c[i]
        hn = rmsnorm(h, mlp_norm[layer])
        h = h + jnp.square(jax.nn.relu(hn @ w_mlp_in[layer])) @ w_mlp_out[layer]

    h = rmsnorm(h, final_norm)
    return h[:, N_META:N_META + seq]
```

```python
import functools
import math

import jax
import jax.numpy as jnp
from jax import lax
from jax.experimental import pallas as pl
from jax.experimental.pallas import tpu as pltpu

D_MODEL = 2048
N_META = 16
CHUNK = 64
HEAD_DIM = 64
EPS = 1e-6
NEG = -1e30

A_HEADS = 16
A_KV_HEADS = 4
A_GROUP = A_HEADS // A_KV_HEADS
WINDOW_CHUNKS = 2
B_HEADS = 16
C_HEADS = D_MODEL // (2 * HEAD_DIM)
D_FF = 4 * D_MODEL

A_Q = A_HEADS * HEAD_DIM
A_KV = A_KV_HEADS * HEAD_DIM
B_W = B_HEADS * HEAD_DIM
AB_IN = A_Q + 2 * A_KV + 3 * B_W
C_QK = C_HEADS * 2 * HEAD_DIM

BLK = 128
FRONT = BLK - N_META
SCALE = HEAD_DIM ** -0.5

VMEM_LIMIT = 56 * 1024 * 1024


def _params(sem, vmem=VMEM_LIMIT):
    return pltpu.CompilerParams(dimension_semantics=sem, vmem_limit_bytes=vmem)


def _chunk_of(u):
    return jnp.where(u < BLK, 0, (u - CHUNK) >> 6)


def _norm_matmul_kernel(x_ref, g_ref, w_ref, o_ref, hn_ref, *, relu2):
    @pl.when(pl.program_id(1) == 0)
    def _():
        x = x_ref[...]
        ms = jnp.mean(x * x, axis=-1, keepdims=True)
        hn_ref[...] = (x * lax.rsqrt(ms + EPS) * g_ref[...]).astype(hn_ref.dtype)

    acc = jnp.dot(hn_ref[...], w_ref[...], preferred_element_type=jnp.float32)
    if relu2:
        acc = jnp.square(jnp.maximum(acc, 0.0))
    o_ref[...] = acc.astype(o_ref.dtype)


def _norm_matmul(x, g, w, *, relu2, tm, tn):
    t, k = x.shape
    n = w.shape[1]
    assert t % tm == 0 and n % tn == 0
    return pl.pallas_call(
        functools.partial(_norm_matmul_kernel, relu2=relu2),
        grid=(t // tm, n // tn),
        in_specs=[
            pl.BlockSpec((tm, k), lambda i, j: (i, 0)),
            pl.BlockSpec((1, k), lambda i, j: (0, 0)),
            pl.BlockSpec((k, tn), lambda i, j: (0, j)),
        ],
        out_specs=pl.BlockSpec((tm, tn), lambda i, j: (i, j)),
        out_shape=jax.ShapeDtypeStruct((t, n), jnp.bfloat16),
        scratch_shapes=[pltpu.VMEM((tm, k), jnp.bfloat16)],
        compiler_params=_params(("arbitrary", "arbitrary")),
        name="norm_matmul_relu2" if relu2 else "norm_matmul",
    )(x, g.reshape(1, k), w)


def _matmul_res_kernel(a_ref, w_ref, r_ref, o_ref):
    part = jnp.dot(a_ref[...], w_ref[...], preferred_element_type=jnp.float32)

    @pl.when(pl.program_id(2) == 0)
    def _():
        o_ref[...] = r_ref[...] + part

    @pl.when(pl.program_id(2) != 0)
    def _():
        o_ref[...] += part


def _matmul_res(a, w, res, *, tm, tn, tk):
    t, k = a.shape
    n = w.shape[1]
    assert t % tm == 0 and n % tn == 0 and k % tk == 0
    return pl.pallas_call(
        _matmul_res_kernel,
        grid=(t // tm, n // tn, k // tk),
        in_specs=[
            pl.BlockSpec((tm, tk), lambda i, j, kk: (i, kk)),
            pl.BlockSpec((tk, tn), lambda i, j, kk: (kk, j)),
            pl.BlockSpec((tm, tn), lambda i, j, kk: (i, j)),
        ],
        out_specs=pl.BlockSpec((tm, tn), lambda i, j, kk: (i, j)),
        out_shape=jax.ShapeDtypeStruct((t, n), jnp.float32),
        compiler_params=_params(("arbitrary", "arbitrary", "arbitrary")),
        name="matmul_res",
    )(a, w, res)


def _final_norm_kernel(x_ref, g_ref, o_ref):
    x = x_ref[0]
    ms = jnp.mean(x * x, axis=-1, keepdims=True)
    o_ref[0] = x * lax.rsqrt(ms + EPS) * g_ref[...]


def _final_norm(h3, g, seq):
    bsz, lp, d = h3.shape
    return pl.pallas_call(
        _final_norm_kernel,
        grid=(bsz, seq // BLK),
        in_specs=[
            pl.BlockSpec((1, BLK, d), lambda b, r: (b, r + 1, 0)),
            pl.BlockSpec((1, d), lambda b, r: (0, 0)),
        ],
        out_specs=pl.BlockSpec((1, BLK, d), lambda b, r: (b, r, 0)),
        out_shape=jax.ShapeDtypeStruct((bsz, seq, d), jnp.float32),
        compiler_params=_params(("arbitrary", "arbitrary")),
        name="final_norm",
    )(h3, g.reshape(1, d))


def _attn_a_kernel(slopes_ref, sinks_ref, q_ref, k_ref, v_ref, o_ref):
    m = pl.program_id(1)
    start = jnp.maximum(m - 1, 0) * BLK
    start = pl.multiple_of(start, BLK)
    nk = 3 * BLK

    q = q_ref[...]
    k_meta = k_ref[pl.ds(0, BLK), :]
    v_meta = v_ref[pl.ds(0, BLK), :]
    k_band = k_ref[pl.ds(start, 2 * BLK), :]
    v_band = v_ref[pl.ds(start, 2 * BLK), :]
    k_all = jnp.concatenate([k_meta, k_band], axis=0)
    v_all = jnp.concatenate([v_meta, v_band], axis=0)

    rows = A_GROUP * BLK
    u_q = m * BLK + (lax.broadcasted_iota(jnp.int32, (rows, nk), 0) & (BLK - 1))
    col = lax.broadcasted_iota(jnp.int32, (rows, nk), 1)
    u_k = jnp.where(col < BLK, col, start + col - BLK)
    qc = _chunk_of(u_q)
    kc = _chunk_of(u_k)
    meta_ok = (col < BLK) & (u_k >= FRONT)
    band_ok = (col >= BLK) & (u_k >= BLK) & (kc <= qc) & (kc >= qc - WINDOW_CHUNKS)
    ok = meta_ok | band_ok
    dist = jnp.abs(u_q - u_k).astype(jnp.float32)
    head_in_group = lax.broadcasted_iota(jnp.int32, (rows, 1), 0) >> 7

    outs = []
    for g in range(A_KV_HEADS):
        k_g = k_all[:, g * HEAD_DIM:(g + 1) * HEAD_DIM]
        v_g = v_all[:, g * HEAD_DIM:(g + 1) * HEAD_DIM]
        q_g = jnp.concatenate(
            [q[:, (g * A_GROUP + a) * HEAD_DIM:(g * A_GROUP + a + 1) * HEAD_DIM]
             for a in range(A_GROUP)], axis=0)
        slope = jnp.zeros((rows, 1), jnp.float32)
        sink = jnp.zeros((rows, 1), jnp.float32)
        for a in range(A_GROUP):
            h = g * A_GROUP + a
            slope = jnp.where(head_in_group == a, slopes_ref[h], slope)
            sink = jnp.where(head_in_group == a, sinks_ref[h], sink)
        s = lax.dot_general(q_g, k_g, (((1,), (1,)), ((), ())),
                            preferred_element_type=jnp.float32) * SCALE
        s = jnp.where(ok, s - slope * dist, NEG)
        mx = jnp.maximum(jnp.max(s, axis=-1, keepdims=True), sink)
        e = jnp.exp(s - mx)
        denom = jnp.sum(e, axis=-1, keepdims=True) + jnp.exp(sink - mx)
        p = (e / denom).astype(jnp.bfloat16)
        o_g = jnp.dot(p, v_g, preferred_element_type=jnp.float32)
        for a in range(A_GROUP):
            outs.append(o_g[a * BLK:(a + 1) * BLK])
    o_ref[...] = jnp.concatenate(outs, axis=1).astype(o_ref.dtype)


def _attn_a(proj, slopes, sinks, bsz, lp):
    nb = lp // BLK
    t = proj.shape[0]
    kcol = A_Q // A_KV
    return pl.pallas_call(
        _attn_a_kernel,
        grid=(bsz, nb),
        in_specs=[
            pl.BlockSpec(memory_space=pltpu.SMEM),
            pl.BlockSpec(memory_space=pltpu.SMEM),
            pl.BlockSpec((BLK, A_Q), lambda b, m: (b * nb + m, 0)),
            pl.BlockSpec((lp, A_KV), lambda b, m: (b, kcol)),
            pl.BlockSpec((lp, A_KV), lambda b, m: (b, kcol + 1)),
        ],
        out_specs=pl.BlockSpec((BLK, A_Q), lambda b, m: (b * nb + m, 0)),
        out_shape=jax.ShapeDtypeStruct((t, A_Q), jnp.bfloat16),
        compiler_params=_params(("arbitrary", "arbitrary")),
        name="attn_a",
    )(slopes, sinks, proj, proj, proj)


def _pair_rows(x):
    lane = lax.broadcasted_iota(jnp.int32, x.shape, 1)
    zero = jnp.zeros_like(x)
    return jnp.concatenate([jnp.where(lane < HEAD_DIM, x, zero),
                            jnp.where(lane >= HEAD_DIM, x, zero)], axis=0)


def _attn_b_kernel(q_ref, k_ref, v_ref, o_ref):
    i = pl.program_id(2)
    q = q_ref[...]
    w2 = 2 * BLK

    rj = lax.broadcasted_iota(jnp.int32, (w2, w2), 0)
    cs = lax.broadcasted_iota(jnp.int32, (w2, w2), 1)
    tri = jnp.where(((rj >> 7) == (cs >> 7)) & (rj > cs), 1.0, 0.0).astype(jnp.bfloat16)
    tri2 = jnp.concatenate([tri, tri], axis=0)

    row = lax.broadcasted_iota(jnp.int32, (BLK, w2), 0)
    col = lax.broadcasted_iota(jnp.int32, (BLK, w2), 1) & (BLK - 1)
    u_q = i * BLK + row

    def body(jj, carry):
        acc, run = carry
        j = i - jj
        off = pl.multiple_of(j * BLK, BLK)
        kd = _pair_rows(k_ref[pl.ds(off, BLK), :])
        vd = _pair_rows(v_ref[pl.ds(off, BLK), :])
        z = lax.dot_general(q, kd, (((1,), (1,)), ((), ())),
                            preferred_element_type=jnp.float32) * SCALE
        u_k = j * BLK + col
        ok = (u_k < u_q) & (u_k >= FRONT)
        lk = -(jnp.maximum(z, 0.0) + jnp.log1p(jnp.exp(-jnp.abs(z))))
        lk = jnp.where(ok, lk, 0.0)
        lk_hi = lk.astype(jnp.bfloat16)
        lk_lo = (lk - lk_hi.astype(jnp.float32)).astype(jnp.bfloat16)
        between = jnp.dot(jnp.concatenate([lk_hi, lk_lo], axis=1), tri2,
                          preferred_element_type=jnp.float32)
        wgt = jnp.where(ok, jnp.exp(z + lk + between + run), 0.0)
        acc = acc + jnp.dot(wgt.astype(jnp.bfloat16), vd, preferred_element_type=jnp.float32)
        tot0 = jnp.sum(lk[:, :BLK], axis=-1, keepdims=True)
        tot1 = jnp.sum(lk[:, BLK:], axis=-1, keepdims=True)
        tot = jnp.concatenate([jnp.broadcast_to(tot0, (BLK, BLK)),
                               jnp.broadcast_to(tot1, (BLK, BLK))], axis=1)
        return acc, run + tot

    acc0 = jnp.zeros((BLK, BLK), jnp.float32)
    run0 = jnp.zeros((BLK, w2), jnp.float32)
    acc, _ = lax.fori_loop(0, i + 1, body, (acc0, run0))
    o_ref[...] = acc.astype(o_ref.dtype)


def _attn_b(proj, bsz, lp):
    nb = lp // BLK
    t = proj.shape[0]
    qcol = (A_Q + 2 * A_KV) // BLK
    kcol = qcol + B_W // BLK
    vcol = kcol + B_W // BLK
    return pl.pallas_call(
        _attn_b_kernel,
        grid=(bsz, B_HEADS // 2, nb),
        in_specs=[
            pl.BlockSpec((BLK, BLK), lambda b, p, i: (b * nb + i, qcol + p)),
            pl.BlockSpec((lp, BLK), lambda b, p, i: (b, kcol + p)),
            pl.BlockSpec((lp, BLK), lambda b, p, i: (b, vcol + p)),
        ],
        out_specs=pl.BlockSpec((BLK, BLK), lambda b, p, i: (b * nb + i, p)),
        out_shape=jax.ShapeDtypeStruct((t, B_W), jnp.bfloat16),
        compiler_params=_params(("arbitrary", "arbitrary", "arbitrary")),
        name="attn_b",
    )(proj, proj, proj)


def _attn_c_kernel(slopes_ref, lam_ref, g_ref, q_ref, k_ref, v_ref, o_ref, *, lambda_init):
    h = pl.program_id(1)
    i = pl.program_id(2)
    q = q_ref[...]
    slope = slopes_ref[h]
    w2 = 2 * BLK

    lv = lam_ref[...]
    t1 = jnp.sum(lv[0:1] * lv[1:2], axis=-1, keepdims=True)
    t2 = jnp.sum(lv[2:3] * lv[3:4], axis=-1, keepdims=True)
    lam = jnp.exp(t1) - jnp.exp(t2) + lambda_init

    row = lax.broadcasted_iota(jnp.int32, (BLK, BLK), 0)
    col = lax.broadcasted_iota(jnp.int32, (BLK, BLK), 1)
    u_q = i * BLK + row
    qc = _chunk_of(u_q)

    def body(j, carry):
        m1, m2, l1, l2, a1, a2 = carry
        off = pl.multiple_of(j * BLK, BLK)
        kd = _pair_rows(k_ref[pl.ds(off, BLK), :])
        v = v_ref[pl.ds(off, BLK), :]
        s = lax.dot_general(q, kd, (((1,), (1,)), ((), ())),
                            preferred_element_type=jnp.float32) * SCALE
        u_k = j * BLK + col
        ok = (u_k >= FRONT) & (_chunk_of(u_k) <= qc)
        bias = slope * jnp.abs(u_q - u_k).astype(jnp.float32)
        s1 = jnp.where(ok, s[:, :BLK] - bias, NEG)
        s2 = jnp.where(ok, s[:, BLK:] - bias, NEG)
        n1 = jnp.maximum(m1, jnp.max(s1, axis=-1, keepdims=True))
        n2 = jnp.maximum(m2, jnp.max(s2, axis=-1, keepdims=True))
        e1 = jnp.exp(s1 - n1)
        e2 = jnp.exp(s2 - n2)
        c1 = jnp.exp(m1 - n1)
        c2 = jnp.exp(m2 - n2)
        l1 = c1 * l1 + jnp.sum(e1, axis=-1, keepdims=True)
        l2 = c2 * l2 + jnp.sum(e2, axis=-1, keepdims=True)
        pv = jnp.dot(jnp.concatenate([e1, e2], axis=0).astype(jnp.bfloat16), v,
                     preferred_element_type=jnp.float32)
        a1 = c1 * a1 + pv[:BLK]
        a2 = c2 * a2 + pv[BLK:]
        return n1, n2, l1, l2, a1, a2

    neg = jnp.full((BLK, 1), NEG, jnp.float32)
    zc = jnp.zeros((BLK, 1), jnp.float32)
    za = jnp.zeros((BLK, BLK), jnp.float32)
    _, _, l1, l2, a1, a2 = lax.fori_loop(0, i + 1, body, (neg, neg, zc, zc, za, za))
    o = a1 / l1 - lam * (a2 / l2)
    ms = jnp.mean(o * o, axis=-1, keepdims=True)
    o = o * lax.rsqrt(ms + EPS) * g_ref[...] * (1.0 - lambda_init)
    o_ref[...] = o.astype(o_ref.dtype)


def _attn_c(proj, slopes, lam_vecs, subln_g, lambda_init, bsz, lp):
    nb = lp // BLK
    t = proj.shape[0]
    kcol = C_QK // BLK
    vcol = 2 * kcol
    return pl.pallas_call(
        functools.partial(_attn_c_kernel, lambda_init=lambda_init),
        grid=(bsz, C_HEADS, nb),
        in_specs=[
            pl.BlockSpec(memory_space=pltpu.SMEM),
            pl.BlockSpec((4, HEAD_DIM), lambda b, h, i: (0, 0)),
            pl.BlockSpec((1, 2 * HEAD_DIM), lambda b, h, i: (0, 0)),
            pl.BlockSpec((BLK, BLK), lambda b, h, i: (b * nb + i, h)),
            pl.BlockSpec((lp, BLK), lambda b, h, i: (b, kcol + h)),
            pl.BlockSpec((lp, BLK), lambda b, h, i: (b, vcol + h)),
        ],
        out_specs=pl.BlockSpec((BLK, BLK), lambda b, h, i: (b * nb + i, h)),
        out_shape=jax.ShapeDtypeStruct((t, C_QK), jnp.bfloat16),
        compiler_params=_params(("arbitrary", "arbitrary", "arbitrary")),
        name="attn_c",
    )(slopes, lam_vecs, subln_g.reshape(1, 2 * HEAD_DIM), proj, proj, proj)


def _alibi_slopes(n):
    return jnp.exp2(-8.0 * (jnp.arange(n, dtype=jnp.float32) + 1.0) / n)


def kernel(x, meta_tokens, ab_norm, w_in_ab, attn_sinks, w_out_ab, c_norm, w_in_c,
           diff_lambda, diff_subln, w_out_c, mlp_norm, w_mlp_in, w_mlp_out, final_norm):
    bsz, seq, d = x.shape
    lp = BLK + seq
    assert seq % BLK == 0 and d == D_MODEL
    t = bsz * lp
    bf = jnp.bfloat16
    depth = mlp_norm.shape[0]

    meta = jnp.broadcast_to(meta_tokens.astype(x.dtype)[None], (bsz, N_META, d))
    h = jnp.concatenate([jnp.zeros((bsz, FRONT, d), x.dtype), meta, x], axis=1).reshape(t, d)

    tm = lp // 2

    for layer in range(depth):
        li = layer // 2
        if layer % 2 == 0:
            proj = _norm_matmul(h, ab_norm[li], w_in_ab[li].astype(bf), relu2=False, tm=tm, tn=512)
            out_a = _attn_a(proj, _alibi_slopes(A_HEADS), attn_sinks[li].astype(jnp.float32), bsz, lp)
            out_b = _attn_b(proj, bsz, lp)
            mixed = jnp.concatenate([out_a, out_b], axis=1)
            h = _matmul_res(mixed, w_out_ab[li].astype(bf), h, tm=tm, tn=1024, tk=2048)
        else:
            lambda_init = 0.8 - 0.6 * math.exp(-0.3 * layer)
            proj = _norm_matmul(h, c_norm[li], w_in_c[li].astype(bf), relu2=False, tm=tm, tn=1024)
            out_c = _attn_c(proj, _alibi_slopes(C_HEADS), diff_lambda[li].astype(jnp.float32),
                            diff_subln[li].astype(jnp.float32), lambda_init, bsz, lp)
            h = _matmul_res(out_c, w_out_c[li].astype(bf), h, tm=tm, tn=1024, tk=2048)
        u = _norm_matmul(h, mlp_norm[layer], w_mlp_in[layer].astype(bf), relu2=True, tm=tm, tn=1024)
        h = _matmul_res(u, w_mlp_out[layer].astype(bf), h, tm=tm, tn=1024, tk=2048)

    return _final_norm(h.reshape(bsz, lp, d), final_norm, seq)
```

```python
import functools
import math

import jax
import jax.numpy as jnp
from jax import lax
from jax.experimental import pallas as pl
from jax.experimental.pallas import tpu as pltpu

D_MODEL = 2048
N_META = 16
CHUNK = 64
HEAD_DIM = 64
EPS = 1e-6
NEG = -1e30

A_HEADS = 16
A_KV_HEADS = 4
A_GROUP = A_HEADS // A_KV_HEADS
WINDOW_CHUNKS = 2
B_HEADS = 16
C_HEADS = D_MODEL // (2 * HEAD_DIM)
D_FF = 4 * D_MODEL

A_Q = A_HEADS * HEAD_DIM
A_KV = A_KV_HEADS * HEAD_DIM
B_W = B_HEADS * HEAD_DIM
AB_IN = A_Q + 2 * A_KV + 3 * B_W
C_QK = C_HEADS * 2 * HEAD_DIM

BLK = 128
TQ = 256
NSTREAM = 4
SCALE = HEAD_DIM ** -0.5

VMEM_LIMIT = 56 * 1024 * 1024

F32 = jnp.float32
BF16 = jnp.bfloat16


def _params(sem, vmem=VMEM_LIMIT):
    return pltpu.CompilerParams(dimension_semantics=sem, vmem_limit_bytes=vmem)


def _nt_dot(a, b):
    return lax.dot_general(a, b, (((1,), (1,)), ((), ())), preferred_element_type=F32)


def _split_heads_rows(x):
    lane = lax.broadcasted_iota(jnp.int32, x.shape, 1)
    zero = jnp.zeros_like(x)
    return jnp.concatenate([jnp.where(lane < HEAD_DIM, x, zero),
                            jnp.where(lane >= HEAD_DIM, x, zero)], axis=0)


def _norm_matmul_kernel(x_ref, g_ref, w_ref, o_ref, hn_ref, *, relu2):
    @pl.when(pl.program_id(1) == 0)
    def _():
        x = x_ref[...]
        ms = jnp.mean(x * x, axis=-1, keepdims=True)
        hn_ref[...] = (x * lax.rsqrt(ms + EPS) * g_ref[...]).astype(hn_ref.dtype)

    acc = jnp.dot(hn_ref[...], w_ref[...], preferred_element_type=F32)
    if relu2:
        acc = jnp.square(jnp.maximum(acc, 0.0))
    o_ref[...] = acc.astype(o_ref.dtype)


def _norm_matmul(x, g, w, *, relu2, tm, tn):
    t, k = x.shape
    n = w.shape[1]
    assert t % tm == 0 and n % tn == 0
    return pl.pallas_call(
        functools.partial(_norm_matmul_kernel, relu2=relu2),
        grid=(t // tm, n // tn),
        in_specs=[
            pl.BlockSpec((tm, k), lambda i, j: (i, 0)),
            pl.BlockSpec((1, k), lambda i, j: (0, 0)),
            pl.BlockSpec((k, tn), lambda i, j: (0, j)),
        ],
        out_specs=pl.BlockSpec((tm, tn), lambda i, j: (i, j)),
        out_shape=jax.ShapeDtypeStruct((t, n), BF16),
        scratch_shapes=[pltpu.VMEM((tm, k), BF16)],
        compiler_params=_params(("arbitrary", "arbitrary")),
        name="norm_matmul_relu2" if relu2 else "norm_matmul",
    )(x, g.reshape(1, k), w)


def _matmul_res_kernel(a_ref, w_ref, r_ref, o_ref):
    part = jnp.dot(a_ref[...], w_ref[...], preferred_element_type=F32)

    @pl.when(pl.program_id(2) == 0)
    def _():
        o_ref[...] = r_ref[...] + part

    @pl.when(pl.program_id(2) != 0)
    def _():
        o_ref[...] += part


def _matmul_res(a, w, res, *, tm, tn, tk):
    t, k = a.shape
    n = w.shape[1]
    assert t % tm == 0 and n % tn == 0 and k % tk == 0
    return pl.pallas_call(
        _matmul_res_kernel,
        grid=(t // tm, n // tn, k // tk),
        in_specs=[
            pl.BlockSpec((tm, tk), lambda i, j, kk: (i, kk)),
            pl.BlockSpec((tk, tn), lambda i, j, kk: (kk, j)),
            pl.BlockSpec((tm, tn), lambda i, j, kk: (i, j)),
        ],
        out_specs=pl.BlockSpec((tm, tn), lambda i, j, kk: (i, j)),
        out_shape=jax.ShapeDtypeStruct((t, n), F32),
        compiler_params=_params(("arbitrary", "arbitrary", "arbitrary")),
        name="matmul_res",
    )(a, w, res)


def _final_norm_kernel(x_ref, g_ref, o_ref):
    x = x_ref[0]
    ms = jnp.mean(x * x, axis=-1, keepdims=True)
    o_ref[0] = x * lax.rsqrt(ms + EPS) * g_ref[...]


def _final_norm(h3, g, seq):
    bsz, lp, d = h3.shape
    rows = 2 * BLK
    return pl.pallas_call(
        _final_norm_kernel,
        grid=(bsz, seq // rows),
        in_specs=[
            pl.BlockSpec((1, rows, d), lambda b, r: (b, r, 0)),
            pl.BlockSpec((1, d), lambda b, r: (0, 0)),
        ],
        out_specs=pl.BlockSpec((1, rows, d), lambda b, r: (b, r, 0)),
        out_shape=jax.ShapeDtypeStruct((bsz, seq, d), F32),
        compiler_params=_params(("arbitrary", "arbitrary")),
        name="final_norm",
    )(h3, g.reshape(1, d))


def _attn_a_kernel(slopes_ref, sinks_ref, q_ref, k_ref, v_ref, o_ref, *, seq):
    m = pl.program_id(1)
    start = pl.multiple_of(jnp.maximum(m - 1, 0) * BLK, BLK)
    nk = 3 * BLK

    q = q_ref[0]
    k_all = jnp.concatenate([k_ref[0, pl.ds(seq, BLK), :],
                             k_ref[0, pl.ds(start, 2 * BLK), :]], axis=0)
    v_all = jnp.concatenate([v_ref[0, pl.ds(seq, BLK), :],
                             v_ref[0, pl.ds(start, 2 * BLK), :]], axis=0)

    rows = A_GROUP * BLK
    u_q = m * BLK + (lax.broadcasted_iota(jnp.int32, (rows, nk), 0) & (BLK - 1))
    col = lax.broadcasted_iota(jnp.int32, (rows, nk), 1)
    q_frame = u_q < seq
    pos_q = jnp.where(q_frame, u_q + N_META, u_q - seq)
    qc = jnp.where(q_frame, 1 + (u_q >> 6), 0)
    u_k = start + col - BLK
    kc = 1 + (u_k >> 6)
    is_meta = col < BLK
    pos_k = jnp.where(is_meta, col, u_k + N_META)
    ok = (col < N_META) | ((col >= BLK) & (u_k < seq) & (kc <= qc) & (kc >= qc - WINDOW_CHUNKS))
    dist = jnp.abs(pos_q - pos_k).astype(F32)
    head_in_group = lax.broadcasted_iota(jnp.int32, (rows, 1), 0) >> 7

    outs = []
    for g in range(A_KV_HEADS):
        k_g = k_all[:, g * HEAD_DIM:(g + 1) * HEAD_DIM]
        v_g = v_all[:, g * HEAD_DIM:(g + 1) * HEAD_DIM]
        q_g = jnp.concatenate(
            [q[:, (g * A_GROUP + a) * HEAD_DIM:(g * A_GROUP + a + 1) * HEAD_DIM]
             for a in range(A_GROUP)], axis=0)
        slope = jnp.zeros((rows, 1), F32)
        sink = jnp.zeros((rows, 1), F32)
        for a in range(A_GROUP):
            h = g * A_GROUP + a
            slope = jnp.where(head_in_group == a, slopes_ref[h], slope)
            sink = jnp.where(head_in_group == a, sinks_ref[h], sink)
        s = _nt_dot(q_g, k_g) * SCALE
        s = jnp.where(ok, s - slope * dist, NEG)
        mx = jnp.maximum(jnp.max(s, axis=-1, keepdims=True), sink)
        e = jnp.exp(s - mx)
        denom = jnp.sum(e, axis=-1, keepdims=True) + jnp.exp(sink - mx)
        p = (e / denom).astype(BF16)
        o_g = jnp.dot(p, v_g, preferred_element_type=F32)
        for a in range(A_GROUP):
            outs.append(o_g[a * BLK:(a + 1) * BLK])
    o_ref[0] = jnp.concatenate(outs, axis=1).astype(o_ref.dtype)


def _attn_a(proj, slopes, sinks, seq):
    bsz, lp, _ = proj.shape
    kcol = A_Q // A_KV
    return pl.pallas_call(
        functools.partial(_attn_a_kernel, seq=seq),
        grid=(bsz, lp // BLK),
        in_specs=[
            pl.BlockSpec(memory_space=pltpu.SMEM),
            pl.BlockSpec(memory_space=pltpu.SMEM),
            pl.BlockSpec((1, BLK, A_Q), lambda b, m: (b, m, 0)),
            pl.BlockSpec((1, lp, A_KV), lambda b, m: (b, 0, kcol)),
            pl.BlockSpec((1, lp, A_KV), lambda b, m: (b, 0, kcol + 1)),
        ],
        out_specs=pl.BlockSpec((1, BLK, A_Q), lambda b, m: (b, m, 0)),
        out_shape=jax.ShapeDtypeStruct((bsz, lp, A_Q), BF16),
        compiler_params=_params(("arbitrary", "arbitrary")),
        name="attn_a",
    )(slopes, sinks, proj, proj, proj)


def _attn_b_kernel(q_ref, k_ref, vt_ref, o_ref, run_ref, acc_ref, *, seq):
    i = pl.program_id(2)
    n_ft = seq // TQ
    w = 2 * TQ

    qd = [_split_heads_rows(q_ref[0, :, g * BLK:(g + 1) * BLK] * SCALE)
          for g in range(NSTREAM)]
    run_ref[...] = jnp.zeros_like(run_ref)
    acc_ref[...] = jnp.zeros_like(acc_ref)

    def tri(tk):
        s_idx = lax.broadcasted_iota(jnp.int32, (tk, 2 * tk), 0)
        j_idx = lax.broadcasted_iota(jnp.int32, (tk, 2 * tk), 1) & (tk - 1)
        return jnp.where(j_idx > s_idx, 1.0, 0.0).astype(BF16)

    tri_blk = tri(BLK)

    def tiles(rows_newest_first, tri_op, masks):
        work = [(g, rows, mask) for rows, mask in zip(rows_newest_first, masks)
                for g in range(NSTREAM)]
        zs = [_nt_dot(k_ref[0, rows, g * BLK:(g + 1) * BLK], qd[g])
              for g, rows, _ in work]
        run = [run_ref[g] for g in range(NSTREAM)]
        sps, offs, newers = [], [], []
        for (g, rows, mask), z in zip(work, zs):
            sp = jnp.maximum(z, 0.0) + jnp.log(1.0 + jnp.exp(-jnp.abs(z)))
            if mask is not None:
                sp = jnp.where(mask, sp, 0.0)
            sp_hi = sp.astype(BF16)
            sp_lo = (sp - sp_hi.astype(F32)).astype(BF16)
            newers.append(jnp.dot(tri_op, jnp.concatenate([sp_hi, sp_lo], axis=0),
                                  preferred_element_type=F32))
            sps.append(sp)
            offs.append(run[g])
            run[g] = run[g] + jnp.sum(sp, axis=0, keepdims=True)
        pvs = []
        for (g, rows, mask), z, sp, off, newer in zip(work, zs, sps, offs, newers):
            wgt = jnp.exp(z - (sp + newer + off))
            if mask is not None:
                wgt = jnp.where(mask, wgt, 0.0)
            pvs.append(jnp.dot(vt_ref[0, g * BLK:(g + 1) * BLK, rows], wgt.astype(BF16),
                               preferred_element_type=F32))
        for g in range(NSTREAM):
            acc_ref[g] += sum(pv for (gg, _, _), pv in zip(work, pvs) if gg == g)
            run_ref[g] = run[g]

    @pl.when(i < n_ft)
    def _():
        base = i * TQ
        r = lax.broadcasted_iota(jnp.int32, (BLK, w), 0)
        c = lax.broadcasted_iota(jnp.int32, (BLK, w), 1) & (TQ - 1)
        tiles([pl.ds(pl.multiple_of(base + BLK, BLK), BLK), pl.ds(pl.multiple_of(base, BLK), BLK)],
              tri_blk, [r + BLK < c, r < c])

        def body(jj, carry):
            off = pl.multiple_of(base - (jj + 1) * TQ, TQ)
            tiles([pl.ds(pl.multiple_of(off + BLK, BLK), BLK), pl.ds(off, BLK)], tri_blk,
                  [None, None])
            return carry

        lax.fori_loop(0, i, body, 0)

    r = lax.broadcasted_iota(jnp.int32, (N_META, w), 0)
    c = lax.broadcasted_iota(jnp.int32, (N_META, w), 1) & (TQ - 1)
    tiles([pl.ds(seq, N_META)], tri(N_META), [(i < n_ft) | (r < c)])

    for g in range(NSTREAM):
        acc = acc_ref[g]
        o_t = jnp.concatenate([acc[:HEAD_DIM, :TQ], acc[HEAD_DIM:, TQ:]], axis=0)
        o_ref[0, :, g * BLK:(g + 1) * BLK] = o_t.T.astype(o_ref.dtype)


def _attn_b(proj, vt, seq):
    bsz, lp, _ = proj.shape
    gw = NSTREAM * BLK
    qcol = (A_Q + 2 * A_KV) // gw
    kcol = qcol + B_W // gw
    return pl.pallas_call(
        functools.partial(_attn_b_kernel, seq=seq),
        grid=(bsz, B_W // gw, pl.cdiv(lp, TQ)),
        in_specs=[
            pl.BlockSpec((1, TQ, gw), lambda b, p, i: (b, i, qcol + p)),
            pl.BlockSpec((1, lp, gw), lambda b, p, i: (b, 0, kcol + p)),
            pl.BlockSpec((1, gw, lp), lambda b, p, i: (b, p, 0)),
        ],
        out_specs=pl.BlockSpec((1, TQ, gw), lambda b, p, i: (b, i, p)),
        out_shape=jax.ShapeDtypeStruct((bsz, lp, B_W), BF16),
        scratch_shapes=[pltpu.VMEM((NSTREAM, 1, 2 * TQ), F32),
                        pltpu.VMEM((NSTREAM, BLK, 2 * TQ), F32)],
        compiler_params=_params(("arbitrary", "arbitrary", "arbitrary")),
        name="attn_b",
    )(proj, proj, vt)


def _attn_c_kernel(slopes_ref, lam_ref, g_ref, q_ref, k_ref, vt_ref, o_ref,
                   m_ref, l_ref, acc_ref, *, seq, lambda_init):
    p = pl.program_id(1)
    i = pl.program_id(2)
    n_ft = seq // TQ
    w = 2 * TQ
    slope = [slopes_ref[p * NSTREAM + g] for g in range(NSTREAM)]
    qd = [_split_heads_rows(q_ref[0, :, g * BLK:(g + 1) * BLK] * SCALE)
          for g in range(NSTREAM)]
    meta = pl.ds(seq, N_META)

    def cols(g):
        return slice(g * BLK, (g + 1) * BLK)

    def meta_scores(g, pos_q):
        r = lax.broadcasted_iota(jnp.int32, (N_META, w), 0)
        return (_nt_dot(k_ref[0, meta, cols(g)], qd[g])
                - slope[g] * jnp.abs(pos_q - r).astype(F32))

    cm = lax.broadcasted_iota(jnp.int32, (N_META, w), 1) & (TQ - 1)

    @pl.when(i == n_ft)
    def _():
        for g in range(NSTREAM):
            t = meta_scores(g, cm)
            m0 = jnp.max(t, axis=0, keepdims=True)
            e0 = jnp.exp(t - m0)
            m_ref[g] = m0
            l_ref[g] = jnp.sum(e0, axis=0, keepdims=True)
            acc_ref[g] = jnp.dot(vt_ref[0, cols(g), meta], e0.astype(BF16),
                                 preferred_element_type=F32)

    @pl.when(i < n_ft)
    def _():
        r = lax.broadcasted_iota(jnp.int32, (TQ, w), 0)
        c = lax.broadcasted_iota(jnp.int32, (TQ, w), 1) & (TQ - 1)
        diag = pl.ds(pl.multiple_of(i * TQ, TQ), TQ)
        vis = (r >> 6) <= (c >> 6)
        dist_d = jnp.abs(c - r).astype(F32)
        s_d = [_nt_dot(k_ref[0, diag, cols(g)], qd[g]) for g in range(NSTREAM)]
        t_m = [meta_scores(g, N_META + i * TQ + cm) for g in range(NSTREAM)]
        e_d, e_m = [], []
        for g in range(NSTREAM):
            t_d = jnp.where(vis, s_d[g] - slope[g] * dist_d, NEG)
            m0 = jnp.maximum(jnp.max(t_d, axis=0, keepdims=True),
                             jnp.max(t_m[g], axis=0, keepdims=True))
            e_d.append(jnp.exp(t_d - m0))
            e_m.append(jnp.exp(t_m[g] - m0))
            m_ref[g] = m0
            l_ref[g] = (jnp.sum(e_d[g], axis=0, keepdims=True)
                        + jnp.sum(e_m[g], axis=0, keepdims=True))
        for g in range(NSTREAM):
            acc_ref[g] = (jnp.dot(vt_ref[0, cols(g), diag], e_d[g].astype(BF16),
                                  preferred_element_type=F32)
                          + jnp.dot(vt_ref[0, cols(g), meta], e_m[g].astype(BF16),
                                    preferred_element_type=F32))

        rel = (c - r).astype(F32)

        def body(j, carry):
            rows = pl.ds(pl.multiple_of(j * TQ, TQ), TQ)
            s = [_nt_dot(k_ref[0, rows, cols(g)], qd[g]) for g in range(NSTREAM)]
            es, corrs = [], []
            for g in range(NSTREAM):
                t = s[g] - slope[g] * rel
                cj = slope[g] * ((i - j) * TQ).astype(F32)
                m_old = m_ref[g]
                m_new = jnp.maximum(m_old, jnp.max(t, axis=0, keepdims=True) - cj)
                e = jnp.exp(t - (m_new + cj))
                corr = jnp.exp(m_old - m_new)
                l_ref[g] = corr * l_ref[g] + jnp.sum(e, axis=0, keepdims=True)
                m_ref[g] = m_new
                es.append(e.astype(BF16))
                corrs.append(corr)
            pvs = [jnp.dot(vt_ref[0, cols(g), rows], es[g], preferred_element_type=F32)
                   for g in range(NSTREAM)]
            for g in range(NSTREAM):
                acc_ref[g] = corrs[g] * acc_ref[g] + pvs[g]
            return carry

        lax.fori_loop(0, i, body, 0)

    lv = lam_ref[...]
    t1 = jnp.sum(lv[0:1] * lv[1:2], axis=-1, keepdims=True)
    t2 = jnp.sum(lv[2:3] * lv[3:4], axis=-1, keepdims=True)
    lam = jnp.exp(t1) - jnp.exp(t2) + lambda_init

    for g in range(NSTREAM):
        o_all = acc_ref[g] / l_ref[g]
        o_t = o_all[:, :TQ] - lam * o_all[:, TQ:]
        ms = jnp.mean(o_t * o_t, axis=0, keepdims=True)
        o = (o_t * lax.rsqrt(ms + EPS)).T * g_ref[...] * (1.0 - lambda_init)
        o_ref[0, :, cols(g)] = o.astype(o_ref.dtype)


def _attn_c(proj, vt, slopes, lam_vecs, subln_g, lambda_init, seq):
    bsz, lp, _ = proj.shape
    gw = NSTREAM * BLK
    kcol = C_QK // gw
    return pl.pallas_call(
        functools.partial(_attn_c_kernel, seq=seq, lambda_init=lambda_init),
        grid=(bsz, C_QK // gw, pl.cdiv(lp, TQ)),
        in_specs=[
            pl.BlockSpec(memory_space=pltpu.SMEM),
            pl.BlockSpec((4, HEAD_DIM), lambda b, p, i: (0, 0)),
            pl.BlockSpec((1, 2 * HEAD_DIM), lambda b, p, i: (0, 0)),
            pl.BlockSpec((1, TQ, gw), lambda b, p, i: (b, i, p)),
            pl.BlockSpec((1, lp, gw), lambda b, p, i: (b, 0, kcol + p)),
            pl.BlockSpec((1, gw, lp), lambda b, p, i: (b, p, 0)),
        ],
        out_specs=pl.BlockSpec((1, TQ, gw), lambda b, p, i: (b, i, p)),
        out_shape=jax.ShapeDtypeStruct((bsz, lp, C_QK), BF16),
        scratch_shapes=[pltpu.VMEM((NSTREAM, 1, 2 * TQ), F32),
                        pltpu.VMEM((NSTREAM, 1, 2 * TQ), F32),
                        pltpu.VMEM((NSTREAM, BLK, 2 * TQ), F32)],
        compiler_params=_params(("arbitrary", "arbitrary", "arbitrary")),
        name="attn_c",
    )(slopes, lam_vecs, subln_g.reshape(1, 2 * HEAD_DIM), proj, proj, vt)


def _alibi_slopes(n):
    return jnp.exp2(-8.0 * (jnp.arange(n, dtype=F32) + 1.0) / n)


def kernel(x, meta_tokens, ab_norm, w_in_ab, attn_sinks, w_out_ab, c_norm, w_in_c,
           diff_lambda, diff_subln, w_out_c, mlp_norm, w_mlp_in, w_mlp_out, final_norm):
    bsz, seq, d = x.shape
    lp = seq + BLK
    assert seq % TQ == 0 and d == D_MODEL
    t = bsz * lp
    depth = mlp_norm.shape[0]

    meta = jnp.broadcast_to(meta_tokens.astype(x.dtype)[None], (bsz, N_META, d))
    filler = jnp.zeros((bsz, BLK - N_META, d), x.dtype)
    h = jnp.concatenate([x, meta, filler], axis=1).reshape(t, d)

    tm = lp // 2

    for layer in range(depth):
        li = layer // 2
        if layer % 2 == 0:
            proj = _norm_matmul(h, ab_norm[li], w_in_ab[li].astype(BF16), relu2=False, tm=tm, tn=512)
            proj = proj.reshape(bsz, lp, AB_IN)
            out_a = _attn_a(proj, _alibi_slopes(A_HEADS), attn_sinks[li].astype(F32), seq)
            vt_b = jnp.swapaxes(proj[:, :, AB_IN - B_W:], 1, 2)
            out_b = _attn_b(proj, vt_b, seq)
            mixed = jnp.concatenate([out_a, out_b], axis=-1).reshape(t, A_Q + B_W)
            h = _matmul_res(mixed, w_out_ab[li].astype(BF16), h, tm=tm, tn=1024, tk=2048)
        else:
            lambda_init = 0.8 - 0.6 * math.exp(-0.3 * layer)
            proj = _norm_matmul(h, c_norm[li], w_in_c[li].astype(BF16), relu2=False, tm=tm, tn=1024)
            proj = proj.reshape(bsz, lp, 3 * C_QK)
            vt_c = jnp.swapaxes(proj[:, :, 2 * C_QK:], 1, 2)
            out_c = _attn_c(proj, vt_c, _alibi_slopes(C_HEADS), diff_lambda[li].astype(F32),
                            diff_subln[li].astype(F32), lambda_init, seq)
            h = _matmul_res(out_c.reshape(t, C_QK), w_out_c[li].astype(BF16), h,
                            tm=tm, tn=1024, tk=2048)
        u = _norm_matmul(h, mlp_norm[layer], w_mlp_in[layer].astype(BF16), relu2=True, tm=tm, tn=1024)
        h = _matmul_res(u, w_mlp_out[layer].astype(BF16), h, tm=tm, tn=1024, tk=2048)

    return _final_norm(h.reshape(bsz, lp, d), final_norm, seq)
```

```python
import functools
import math

import jax
import jax.numpy as jnp
from jax import lax
from jax.experimental import pallas as pl
from jax.experimental.pallas import tpu as pltpu

D_MODEL = 2048
N_META = 16
CHUNK = 64
HEAD_DIM = 64
EPS = 1e-6
NEG = -1e30

A_HEADS = 16
A_KV_HEADS = 4
A_GROUP = A_HEADS // A_KV_HEADS
WINDOW_CHUNKS = 2
B_HEADS = 16
C_HEADS = D_MODEL // (2 * HEAD_DIM)
D_FF = 4 * D_MODEL

A_Q = A_HEADS * HEAD_DIM
A_KV = A_KV_HEADS * HEAD_DIM
B_W = B_HEADS * HEAD_DIM
AB_IN = A_Q + 2 * A_KV + 3 * B_W
C_QK = C_HEADS * 2 * HEAD_DIM

BLK = 128
TQ = 256
NSTREAM = 4
ACC_ROWS = BLK + 16
SCALE = HEAD_DIM ** -0.5

VMEM_LIMIT = 56 * 1024 * 1024

F32 = jnp.float32
BF16 = jnp.bfloat16


def _params(sem, vmem=VMEM_LIMIT):
    return pltpu.CompilerParams(dimension_semantics=sem, vmem_limit_bytes=vmem)


def _nt_dot(a, b):
    return lax.dot_general(a, b, (((1,), (1,)), ((), ())), preferred_element_type=F32)


def _split_heads_rows(x):
    lane = lax.broadcasted_iota(jnp.int32, x.shape, 1)
    zero = jnp.zeros_like(x)
    return jnp.concatenate([jnp.where(lane < HEAD_DIM, x, zero),
                            jnp.where(lane >= HEAD_DIM, x, zero)], axis=0)


def _norm_matmul_kernel(x_ref, g_ref, w_ref, o_ref, hn_ref, *, relu2):
    @pl.when(pl.program_id(1) == 0)
    def _():
        x = x_ref[...]
        ms = jnp.mean(x * x, axis=-1, keepdims=True)
        hn_ref[...] = (x * lax.rsqrt(ms + EPS) * g_ref[...]).astype(hn_ref.dtype)

    acc = jnp.dot(hn_ref[...], w_ref[...].astype(BF16), preferred_element_type=F32)
    if relu2:
        acc = jnp.square(jnp.maximum(acc, 0.0))
    o_ref[...] = acc.astype(o_ref.dtype)


def _norm_matmul(x, g, w_stack, li, *, relu2, tm, tn):
    t, k = x.shape
    n = w_stack.shape[2]
    assert t % tm == 0 and n % tn == 0
    return pl.pallas_call(
        functools.partial(_norm_matmul_kernel, relu2=relu2),
        grid=(t // tm, n // tn),
        in_specs=[
            pl.BlockSpec((tm, k), lambda i, j: (i, 0)),
            pl.BlockSpec((1, k), lambda i, j: (0, 0)),
            pl.BlockSpec((None, k, tn), lambda i, j: (li, 0, j)),
        ],
        out_specs=pl.BlockSpec((tm, tn), lambda i, j: (i, j)),
        out_shape=jax.ShapeDtypeStruct((t, n), BF16),
        scratch_shapes=[pltpu.VMEM((tm, k), BF16)],
        compiler_params=_params(("arbitrary", "arbitrary")),
        name="norm_matmul_relu2" if relu2 else "norm_matmul",
    )(x, g.reshape(1, k), w_stack)


def _matmul_res_kernel(*refs, n_a):
    a_refs, (w_ref, r_ref, o_ref) = refs[:n_a], refs[n_a:]
    w = w_ref[...].astype(BF16)
    ka = w.shape[0] // n_a
    part = jnp.dot(a_refs[0][...], w[:ka], preferred_element_type=F32)
    for s in range(1, n_a):
        part += jnp.dot(a_refs[s][...], w[s * ka:(s + 1) * ka], preferred_element_type=F32)

    @pl.when(pl.program_id(2) == 0)
    def _():
        o_ref[...] = r_ref[...] + part

    @pl.when(pl.program_id(2) != 0)
    def _():
        o_ref[...] += part


def _matmul_res(a_parts, w_stack, li, res, *, tm, tn, tk):
    n_a = len(a_parts)
    t = a_parts[0].shape[0]
    _, k, n = w_stack.shape
    assert sum(a.shape[1] for a in a_parts) == k and (n_a == 1 or tk == k)
    assert t % tm == 0 and n % tn == 0 and k % tk == 0
    return pl.pallas_call(
        functools.partial(_matmul_res_kernel, n_a=n_a),
        grid=(t // tm, n // tn, k // tk),
        in_specs=[pl.BlockSpec((tm, tk // n_a), lambda i, j, kk: (i, kk)) for _ in a_parts] + [
            pl.BlockSpec((None, tk, tn), lambda i, j, kk: (li, kk, j)),
            pl.BlockSpec((tm, tn), lambda i, j, kk: (i, j)),
        ],
        out_specs=pl.BlockSpec((tm, tn), lambda i, j, kk: (i, j)),
        out_shape=jax.ShapeDtypeStruct((t, n), F32),
        compiler_params=_params(("arbitrary", "arbitrary", "arbitrary")),
        name="matmul_res",
    )(*a_parts, w_stack, res)


def _final_norm_kernel(x_ref, g_ref, o_ref):
    x = x_ref[0]
    ms = jnp.mean(x * x, axis=-1, keepdims=True)
    o_ref[0] = x * lax.rsqrt(ms + EPS) * g_ref[...]


def _final_norm(h3, g, seq):
    bsz, lp, d = h3.shape
    rows = 2 * BLK
    return pl.pallas_call(
        _final_norm_kernel,
        grid=(bsz, seq // rows),
        in_specs=[
            pl.BlockSpec((1, rows, d), lambda b, r: (b, r, 0)),
            pl.BlockSpec((1, d), lambda b, r: (0, 0)),
        ],
        out_specs=pl.BlockSpec((1, rows, d), lambda b, r: (b, r, 0)),
        out_shape=jax.ShapeDtypeStruct((bsz, seq, d), F32),
        compiler_params=_params(("arbitrary", "arbitrary")),
        name="final_norm",
    )(h3, g.reshape(1, d))


def _attn_a_kernel(slopes_ref, sinks_ref, q_ref, k_ref, v_ref, o_ref, *, seq):
    m = pl.program_id(1)
    start = pl.multiple_of(jnp.maximum(m - 1, 0) * BLK, BLK)
    nk = 3 * BLK

    q = q_ref[0]
    k_all = jnp.concatenate([k_ref[0, pl.ds(seq, BLK), :],
                             k_ref[0, pl.ds(start, 2 * BLK), :]], axis=0)
    v_all = jnp.concatenate([v_ref[0, pl.ds(seq, BLK), :],
                             v_ref[0, pl.ds(start, 2 * BLK), :]], axis=0)

    rows = A_GROUP * BLK
    u_q = m * BLK + (lax.broadcasted_iota(jnp.int32, (rows, nk), 0) & (BLK - 1))
    col = lax.broadcasted_iota(jnp.int32, (rows, nk), 1)
    q_frame = u_q < seq
    pos_q = jnp.where(q_frame, u_q + N_META, u_q - seq)
    qc = jnp.where(q_frame, 1 + (u_q >> 6), 0)
    u_k = start + col - BLK
    kc = 1 + (u_k >> 6)
    is_meta = col < BLK
    pos_k = jnp.where(is_meta, col, u_k + N_META)
    ok = (col < N_META) | ((col >= BLK) & (u_k < seq) & (kc <= qc) & (kc >= qc - WINDOW_CHUNKS))
    dist = jnp.abs(pos_q - pos_k).astype(F32)
    head_in_group = lax.broadcasted_iota(jnp.int32, (rows, 1), 0) >> 7

    outs = []
    for g in range(A_KV_HEADS):
        k_g = k_all[:, g * HEAD_DIM:(g + 1) * HEAD_DIM]
        v_g = v_all[:, g * HEAD_DIM:(g + 1) * HEAD_DIM]
        q_g = jnp.concatenate(
            [q[:, (g * A_GROUP + a) * HEAD_DIM:(g * A_GROUP + a + 1) * HEAD_DIM]
             for a in range(A_GROUP)], axis=0)
        slope = jnp.zeros((rows, 1), F32)
        sink = jnp.zeros((rows, 1), F32)
        for a in range(A_GROUP):
            h = g * A_GROUP + a
            slope = jnp.where(head_in_group == a, slopes_ref[h], slope)
            sink = jnp.where(head_in_group == a, sinks_ref[h], sink)
        s = _nt_dot(q_g, k_g) * SCALE
        s = jnp.where(ok, s - slope * dist, NEG)
        mx = jnp.maximum(jnp.max(s, axis=-1, keepdims=True), sink)
        e = jnp.exp(s - mx)
        denom = jnp.sum(e, axis=-1, keepdims=True) + jnp.exp(sink - mx)
        p = (e / denom).astype(BF16)
        o_g = jnp.dot(p, v_g, preferred_element_type=F32)
        for a in range(A_GROUP):
            outs.append(o_g[a * BLK:(a + 1) * BLK])
    o_ref[0] = jnp.concatenate(outs, axis=1).astype(o_ref.dtype)


def _attn_a(proj, slopes, sinks, seq):
    bsz, lp, _ = proj.shape
    kcol = A_Q // A_KV
    return pl.pallas_call(
        functools.partial(_attn_a_kernel, seq=seq),
        grid=(bsz, lp // BLK),
        in_specs=[
            pl.BlockSpec(memory_space=pltpu.SMEM),
            pl.BlockSpec(memory_space=pltpu.SMEM),
            pl.BlockSpec((1, BLK, A_Q), lambda b, m: (b, m, 0)),
            pl.BlockSpec((1, lp, A_KV), lambda b, m: (b, 0, kcol)),
            pl.BlockSpec((1, lp, A_KV), lambda b, m: (b, 0, kcol + 1)),
        ],
        out_specs=pl.BlockSpec((1, BLK, A_Q), lambda b, m: (b, m, 0)),
        out_shape=jax.ShapeDtypeStruct((bsz, lp, A_Q), BF16),
        compiler_params=_params(("arbitrary", "arbitrary")),
        name="attn_a",
    )(slopes, sinks, proj, proj, proj)


def _attn_b_kernel(q_ref, k_ref, vt_ref, o_ref, run_ref, acc_ref, *, seq):
    i = pl.program_id(2)
    n_ft = seq // TQ
    w = 2 * TQ

    qd = [_split_heads_rows(q_ref[0, :, g * BLK:(g + 1) * BLK] * SCALE)
          for g in range(NSTREAM)]
    run_ref[...] = jnp.zeros_like(run_ref)
    acc_ref[...] = jnp.zeros_like(acc_ref)

    def tri(tk):
        s_idx = lax.broadcasted_iota(jnp.int32, (tk, 2 * tk), 0)
        j_idx = lax.broadcasted_iota(jnp.int32, (tk, 2 * tk), 1) & (tk - 1)
        return jnp.where(j_idx > s_idx, 1.0, 0.0).astype(BF16)

    tri_blk = tri(BLK)

    def tiles(rows_newest_first, tri_op, masks):
        work = [(g, rows, mask) for rows, mask in zip(rows_newest_first, masks)
                for g in range(NSTREAM)]
        zs = [_nt_dot(k_ref[0, rows, g * BLK:(g + 1) * BLK], qd[g])
              for g, rows, _ in work]
        run = [run_ref[g] for g in range(NSTREAM)]
        sps, offs, newers = [], [], []
        for (g, rows, mask), z in zip(work, zs):
            sp = jnp.maximum(z, 0.0) + jnp.log(1.0 + jnp.exp(-jnp.abs(z)))
            if mask is not None:
                sp = jnp.where(mask, sp, 0.0)
            sp_hi = sp.astype(BF16)
            sp_lo = (sp - sp_hi.astype(F32)).astype(BF16)
            newers.append(jnp.dot(tri_op, jnp.concatenate([sp_hi, sp_lo], axis=0),
                                  preferred_element_type=F32))
            sps.append(sp)
            offs.append(run[g])
            run[g] = run[g] + jnp.sum(sp, axis=0, keepdims=True)
        pvs = []
        for (g, rows, mask), z, sp, off, newer in zip(work, zs, sps, offs, newers):
            wgt = jnp.exp(z - (sp + newer + off))
            if mask is not None:
                wgt = jnp.where(mask, wgt, 0.0)
            pvs.append(jnp.dot(vt_ref[0, g * BLK:(g + 1) * BLK, rows], wgt.astype(BF16),
                               preferred_element_type=F32))
        for g in range(NSTREAM):
            acc_ref[g] += sum(pv for (gg, _, _), pv in zip(work, pvs) if gg == g)
            run_ref[g] = run[g]

    @pl.when(i < n_ft)
    def _():
        base = i * TQ
        r = lax.broadcasted_iota(jnp.int32, (BLK, w), 0)
        c = lax.broadcasted_iota(jnp.int32, (BLK, w), 1) & (TQ - 1)
        tiles([pl.ds(pl.multiple_of(base + BLK, BLK), BLK), pl.ds(pl.multiple_of(base, BLK), BLK)],
              tri_blk, [r + BLK < c, r < c])

        def body(jj, carry):
            off = pl.multiple_of(base - (jj + 1) * TQ, TQ)
            tiles([pl.ds(pl.multiple_of(off + BLK, BLK), BLK), pl.ds(off, BLK)], tri_blk,
                  [None, None])
            return carry

        lax.fori_loop(0, i, body, 0)

    r = lax.broadcasted_iota(jnp.int32, (N_META, w), 0)
    c = lax.broadcasted_iota(jnp.int32, (N_META, w), 1) & (TQ - 1)
    tiles([pl.ds(seq, N_META)], tri(N_META), [(i < n_ft) | (r < c)])

    for g in range(NSTREAM):
        acc = acc_ref[g]
        o_t = jnp.concatenate([acc[:HEAD_DIM, :TQ], acc[HEAD_DIM:, TQ:]], axis=0)
        o_ref[0, :, g * BLK:(g + 1) * BLK] = o_t.T.astype(o_ref.dtype)


def _attn_b(proj, vt, seq):
    bsz, lp, _ = proj.shape
    gw = NSTREAM * BLK
    qcol = (A_Q + 2 * A_KV) // gw
    kcol = qcol + B_W // gw
    return pl.pallas_call(
        functools.partial(_attn_b_kernel, seq=seq),
        grid=(bsz, B_W // gw, pl.cdiv(lp, TQ)),
        in_specs=[
            pl.BlockSpec((1, TQ, gw), lambda b, p, i: (b, i, qcol + p)),
            pl.BlockSpec((1, lp, gw), lambda b, p, i: (b, 0, kcol + p)),
            pl.BlockSpec((1, gw, lp), lambda b, p, i: (b, p, 0)),
        ],
        out_specs=pl.BlockSpec((1, TQ, gw), lambda b, p, i: (b, i, p)),
        out_shape=jax.ShapeDtypeStruct((bsz, lp, B_W), BF16),
        scratch_shapes=[pltpu.VMEM((NSTREAM, 1, 2 * TQ), F32),
                        pltpu.VMEM((NSTREAM, BLK, 2 * TQ), F32)],
        compiler_params=_params(("arbitrary", "arbitrary", "arbitrary")),
        name="attn_b",
    )(proj, proj, vt)


def _attn_c_kernel(slopes_ref, lam_ref, g_ref, q_ref, k_ref, vt_ref, o_ref,
                   m_ref, acc_ref, *, seq, lambda_init):
    p = pl.program_id(1)
    i = pl.program_id(2)
    n_ft = seq // TQ
    w = 2 * TQ
    slope = [slopes_ref[p * NSTREAM + g] for g in range(NSTREAM)]
    qd = [_split_heads_rows(q_ref[0, :, g * BLK:(g + 1) * BLK] * SCALE)
          for g in range(NSTREAM)]
    meta = pl.ds(seq, N_META)

    def cols(g):
        return slice(g * BLK, (g + 1) * BLK)

    def vt_ones(g, rows):
        tk = rows.size
        return jnp.concatenate([vt_ref[0, cols(g), rows], jnp.ones((ACC_ROWS - BLK, tk), BF16)],
                               axis=0)

    def meta_scores(g):
        return _nt_dot(k_ref[0, meta, cols(g)], qd[g])

    rm = lax.broadcasted_iota(jnp.int32, (N_META, w), 0)
    cm = lax.broadcasted_iota(jnp.int32, (N_META, w), 1) & (TQ - 1)

    @pl.when(i == n_ft)
    def _():
        dist = jnp.abs(cm - rm).astype(F32)
        for g in range(NSTREAM):
            t = meta_scores(g) - slope[g] * dist
            m0 = jnp.max(t, axis=0, keepdims=True)
            e0 = jnp.exp(t - m0)
            m_ref[g] = m0
            acc_ref[g] = jnp.dot(vt_ones(g, meta), e0.astype(BF16), preferred_element_type=F32)

    @pl.when(i < n_ft)
    def _():
        r = lax.broadcasted_iota(jnp.int32, (TQ, w), 0)
        c = lax.broadcasted_iota(jnp.int32, (TQ, w), 1) & (TQ - 1)
        lane_k = lax.broadcasted_iota(jnp.int32, (TQ, BLK), 1)
        row_k = lax.broadcasted_iota(jnp.int32, (TQ, BLK), 0)
        key_off = jnp.where(lane_k < 3, row_k, 0).astype(F32).astype(BF16)
        lane_q = lax.broadcasted_iota(jnp.int32, (w, BLK), 1)
        qa = []
        for g in range(NSTREAM):
            sl = jnp.full((w, BLK), slope[g], F32)
            s_hi = sl.astype(BF16).astype(F32)
            s_mid = (sl - s_hi).astype(BF16).astype(F32)
            s_lo = (sl - s_hi) - s_mid
            sl3 = jnp.where(lane_q == 0, s_hi,
                            jnp.where(lane_q == 1, s_mid, jnp.where(lane_q == 2, s_lo, 0.0)))
            qa.append(jnp.concatenate([qd[g], sl3.astype(BF16)], axis=1))

        def scores(g, rows):
            return _nt_dot(jnp.concatenate([k_ref[0, rows, cols(g)], key_off], axis=1), qa[g])

        diag = pl.ds(pl.multiple_of(i * TQ, TQ), TQ)
        vis = (r >> 6) <= (c >> 6)
        ahead = 2.0 * jnp.maximum(r - c, 0).astype(F32)
        meta_off = (rm - (N_META + i * TQ)).astype(F32)
        s_d = [scores(g, diag) for g in range(NSTREAM)]
        s_m = [meta_scores(g) for g in range(NSTREAM)]
        e_d, e_m = [], []
        for g in range(NSTREAM):
            t_d = jnp.where(vis, s_d[g] - slope[g] * ahead, NEG)
            t_m = s_m[g] + slope[g] * meta_off
            m0 = jnp.maximum(jnp.max(t_d, axis=0, keepdims=True),
                             jnp.max(t_m, axis=0, keepdims=True))
            e_d.append(jnp.exp(t_d - m0).astype(BF16))
            e_m.append(jnp.exp(t_m - m0).astype(BF16))
            m_ref[g] = m0
        for g in range(NSTREAM):
            acc_ref[g] = (jnp.dot(vt_ones(g, diag), e_d[g], preferred_element_type=F32)
                          + jnp.dot(vt_ones(g, meta), e_m[g], preferred_element_type=F32))

        def body(j, carry):
            rows = pl.ds(pl.multiple_of(j * TQ, TQ), TQ)
            s = [scores(g, rows) for g in range(NSTREAM)]
            es, corrs = [], []
            for g in range(NSTREAM):
                cj = slope[g] * ((i - j) * TQ).astype(F32)
                m_old = m_ref[g]
                m_new = jnp.maximum(m_old, jnp.max(s[g], axis=0, keepdims=True) - cj)
                es.append(jnp.exp(s[g] - (m_new + cj)).astype(BF16))
                corrs.append(jnp.exp(m_old - m_new))
                m_ref[g] = m_new
            pvs = [jnp.dot(vt_ones(g, rows), es[g], preferred_element_type=F32)
                   for g in range(NSTREAM)]
            for g in range(NSTREAM):
                acc_ref[g] = corrs[g] * acc_ref[g] + pvs[g]
            return carry

        lax.fori_loop(0, i, body, 0)

    lv = lam_ref[...]
    t1 = jnp.sum(lv[0:1] * lv[1:2], axis=-1, keepdims=True)
    t2 = jnp.sum(lv[2:3] * lv[3:4], axis=-1, keepdims=True)
    lam = jnp.exp(t1) - jnp.exp(t2) + lambda_init

    for g in range(NSTREAM):
        acc = acc_ref[g]
        o_all = acc[:BLK] / acc[BLK:BLK + 1]
        o_t = o_all[:, :TQ] - lam * o_all[:, TQ:]
        ms = jnp.mean(o_t * o_t, axis=0, keepdims=True)
        o = (o_t * lax.rsqrt(ms + EPS)).T * g_ref[...] * (1.0 - lambda_init)
        o_ref[0, :, cols(g)] = o.astype(o_ref.dtype)


def _attn_c(proj, vt, slopes, lam_vecs, subln_g, lambda_init, seq):
    bsz, lp, _ = proj.shape
    gw = NSTREAM * BLK
    kcol = C_QK // gw
    return pl.pallas_call(
        functools.partial(_attn_c_kernel, seq=seq, lambda_init=lambda_init),
        grid=(bsz, C_QK // gw, pl.cdiv(lp, TQ)),
        in_specs=[
            pl.BlockSpec(memory_space=pltpu.SMEM),
            pl.BlockSpec((4, HEAD_DIM), lambda b, p, i: (0, 0)),
            pl.BlockSpec((1, 2 * HEAD_DIM), lambda b, p, i: (0, 0)),
            pl.BlockSpec((1, TQ, gw), lambda b, p, i: (b, i, p)),
            pl.BlockSpec((1, lp, gw), lambda b, p, i: (b, 0, kcol + p)),
            pl.BlockSpec((1, gw, lp), lambda b, p, i: (b, p, 0)),
        ],
        out_specs=pl.BlockSpec((1, TQ, gw), lambda b, p, i: (b, i, p)),
        out_shape=jax.ShapeDtypeStruct((bsz, lp, C_QK), BF16),
        scratch_shapes=[pltpu.VMEM((NSTREAM, 1, 2 * TQ), F32),
                        pltpu.VMEM((NSTREAM, ACC_ROWS, 2 * TQ), F32)],
        compiler_params=_params(("arbitrary", "arbitrary", "arbitrary")),
        name="attn_c",
    )(slopes, lam_vecs, subln_g.reshape(1, 2 * HEAD_DIM), proj, proj, vt)


def _alibi_slopes(n):
    return jnp.exp2(-8.0 * (jnp.arange(n, dtype=F32) + 1.0) / n)


def kernel(x, meta_tokens, ab_norm, w_in_ab, attn_sinks, w_out_ab, c_norm, w_in_c,
           diff_lambda, diff_subln, w_out_c, mlp_norm, w_mlp_in, w_mlp_out, final_norm):
    bsz, seq, d = x.shape
    lp = seq + BLK
    assert seq % TQ == 0 and d == D_MODEL
    t = bsz * lp
    depth = mlp_norm.shape[0]

    meta = jnp.broadcast_to(meta_tokens.astype(x.dtype)[None], (bsz, N_META, d))
    filler = jnp.zeros((bsz, BLK - N_META, d), x.dtype)
    h = jnp.concatenate([x, meta, filler], axis=1).reshape(t, d)

    tm = lp // 2

    for layer in range(depth):
        li = layer // 2
        if layer % 2 == 0:
            proj = _norm_matmul(h, ab_norm[li], w_in_ab, li, relu2=False, tm=tm, tn=512)
            proj = proj.reshape(bsz, lp, AB_IN)
            out_a = _attn_a(proj, _alibi_slopes(A_HEADS), attn_sinks[li].astype(F32), seq)
            vt_b = jnp.swapaxes(proj[:, :, AB_IN - B_W:], 1, 2)
            out_b = _attn_b(proj, vt_b, seq)
            h = _matmul_res([out_a.reshape(t, A_Q), out_b.reshape(t, B_W)], w_out_ab, li, h,
                            tm=tm, tn=1024, tk=A_Q + B_W)
        else:
            lambda_init = 0.8 - 0.6 * math.exp(-0.3 * layer)
            proj = _norm_matmul(h, c_norm[li], w_in_c, li, relu2=False, tm=tm, tn=1024)
            proj = proj.reshape(bsz, lp, 3 * C_QK)
            vt_c = jnp.swapaxes(proj[:, :, 2 * C_QK:], 1, 2)
            out_c = _attn_c(proj, vt_c, _alibi_slopes(C_HEADS), diff_lambda[li].astype(F32),
                            diff_subln[li].astype(F32), lambda_init, seq)
            h = _matmul_res([out_c.reshape(t, C_QK)], w_out_c, li, h, tm=tm, tn=1024, tk=2048)
        u = _norm_matmul(h, mlp_norm[layer], w_mlp_in, layer, relu2=True, tm=tm, tn=1024)
        h = _matmul_res([u], w_mlp_out, layer, h, tm=tm, tn=1024, tk=2048)

    return _final_norm(h.reshape(bsz, lp, d), final_norm, seq)
```

```python
import functools
import math

import jax
import jax.numpy as jnp
from jax import lax
from jax.experimental import pallas as pl
from jax.experimental.pallas import tpu as pltpu

D_MODEL = 2048
N_META = 16
CHUNK = 64
HEAD_DIM = 64
EPS = 1e-6
NEG = -1e30

A_HEADS = 16
A_KV_HEADS = 4
A_GROUP = A_HEADS // A_KV_HEADS
WINDOW_CHUNKS = 2
B_HEADS = 16
C_HEADS = D_MODEL // (2 * HEAD_DIM)
D_FF = 4 * D_MODEL

A_Q = A_HEADS * HEAD_DIM
A_KV = A_KV_HEADS * HEAD_DIM
B_W = B_HEADS * HEAD_DIM
AB_IN = A_Q + 2 * A_KV + 3 * B_W
C_QK = C_HEADS * 2 * HEAD_DIM

BLK = 128
TQ = 256
NSTREAM = 4
ACC_ROWS = BLK + 16
SCALE = HEAD_DIM ** -0.5

VMEM_LIMIT = 56 * 1024 * 1024

F32 = jnp.float32
BF16 = jnp.bfloat16


def _params(sem, vmem=VMEM_LIMIT):
    return pltpu.CompilerParams(dimension_semantics=sem, vmem_limit_bytes=vmem)


def _nt_dot(a, b):
    return lax.dot_general(a, b, (((1,), (1,)), ((), ())), preferred_element_type=F32)


def _split_heads_rows(x):
    lane = lax.broadcasted_iota(jnp.int32, x.shape, 1)
    zero = jnp.zeros_like(x)
    return jnp.concatenate([jnp.where(lane < HEAD_DIM, x, zero),
                            jnp.where(lane >= HEAD_DIM, x, zero)], axis=0)


def _norm_matmul_kernel(x_ref, g_ref, w_ref, o_ref, hn_ref, *, relu2):
    @pl.when(pl.program_id(1) == 0)
    def _():
        x = x_ref[...]
        ms = jnp.mean(x * x, axis=-1, keepdims=True)
        hn_ref[...] = (x * lax.rsqrt(ms + EPS) * g_ref[...]).astype(hn_ref.dtype)

    acc = jnp.dot(hn_ref[...], w_ref[...].astype(BF16), preferred_element_type=F32)
    if relu2:
        acc = jnp.square(jnp.maximum(acc, 0.0))
    o_ref[...] = acc.astype(o_ref.dtype)


def _norm_matmul(x, g, w_stack, li, *, relu2, tm, tn):
    t, k = x.shape
    n = w_stack.shape[2]
    assert t % tm == 0 and n % tn == 0
    return pl.pallas_call(
        functools.partial(_norm_matmul_kernel, relu2=relu2),
        grid=(t // tm, n // tn),
        in_specs=[
            pl.BlockSpec((tm, k), lambda i, j: (i, 0)),
            pl.BlockSpec((1, k), lambda i, j: (0, 0)),
            pl.BlockSpec((None, k, tn), lambda i, j: (li, 0, j)),
        ],
        out_specs=pl.BlockSpec((tm, tn), lambda i, j: (i, j)),
        out_shape=jax.ShapeDtypeStruct((t, n), BF16),
        scratch_shapes=[pltpu.VMEM((tm, k), BF16)],
        compiler_params=_params(("arbitrary", "arbitrary")),
        name="norm_matmul_relu2" if relu2 else "norm_matmul",
    )(x, g.reshape(1, k), w_stack)


def _matmul_res_kernel(*refs, n_a):
    a_refs, (w_ref, r_ref, o_ref) = refs[:n_a], refs[n_a:]
    w = w_ref[...].astype(BF16)
    ka = w.shape[0] // n_a
    part = jnp.dot(a_refs[0][...], w[:ka], preferred_element_type=F32)
    for s in range(1, n_a):
        part += jnp.dot(a_refs[s][...], w[s * ka:(s + 1) * ka], preferred_element_type=F32)

    @pl.when(pl.program_id(2) == 0)
    def _():
        o_ref[...] = r_ref[...] + part

    @pl.when(pl.program_id(2) != 0)
    def _():
        o_ref[...] += part


def _matmul_res(a_parts, w_stack, li, res, *, tm, tn, tk):
    n_a = len(a_parts)
    t = a_parts[0].shape[0]
    _, k, n = w_stack.shape
    assert sum(a.shape[1] for a in a_parts) == k and (n_a == 1 or tk == k)
    assert t % tm == 0 and n % tn == 0 and k % tk == 0
    return pl.pallas_call(
        functools.partial(_matmul_res_kernel, n_a=n_a),
        grid=(t // tm, n // tn, k // tk),
        in_specs=[pl.BlockSpec((tm, tk // n_a), lambda i, j, kk: (i, kk)) for _ in a_parts] + [
            pl.BlockSpec((None, tk, tn), lambda i, j, kk: (li, kk, j)),
            pl.BlockSpec((tm, tn), lambda i, j, kk: (i, j)),
        ],
        out_specs=pl.BlockSpec((tm, tn), lambda i, j, kk: (i, j)),
        out_shape=jax.ShapeDtypeStruct((t, n), F32),
        compiler_params=_params(("arbitrary", "arbitrary", "arbitrary")),
        name="matmul_res",
    )(*a_parts, w_stack, res)


def _final_norm_kernel(x_ref, g_ref, o_ref):
    x = x_ref[0]
    ms = jnp.mean(x * x, axis=-1, keepdims=True)
    o_ref[0] = x * lax.rsqrt(ms + EPS) * g_ref[...]


def _final_norm(h3, g, seq):
    bsz, lp, d = h3.shape
    rows = 2 * BLK
    return pl.pallas_call(
        _final_norm_kernel,
        grid=(bsz, seq // rows),
        in_specs=[
            pl.BlockSpec((1, rows, d), lambda b, r: (b, r, 0)),
            pl.BlockSpec((1, d), lambda b, r: (0, 0)),
        ],
        out_specs=pl.BlockSpec((1, rows, d), lambda b, r: (b, r, 0)),
        out_shape=jax.ShapeDtypeStruct((bsz, seq, d), F32),
        compiler_params=_params(("arbitrary", "arbitrary")),
        name="final_norm",
    )(h3, g.reshape(1, d))


def _attn_a_kernel(slopes_ref, sinks_ref, q_ref, k_ref, v_ref, o_ref, *, seq):
    m = pl.program_id(1)
    start = pl.multiple_of(jnp.maximum(m - 1, 0) * BLK, BLK)
    nk = 3 * BLK

    q = q_ref[0]
    k_all = jnp.concatenate([k_ref[0, pl.ds(seq, BLK), :],
                             k_ref[0, pl.ds(start, 2 * BLK), :]], axis=0)
    v_all = jnp.concatenate([v_ref[0, pl.ds(seq, BLK), :],
                             v_ref[0, pl.ds(start, 2 * BLK), :]], axis=0)

    rows = A_GROUP * BLK
    u_q = m * BLK + (lax.broadcasted_iota(jnp.int32, (rows, nk), 0) & (BLK - 1))
    col = lax.broadcasted_iota(jnp.int32, (rows, nk), 1)
    q_frame = u_q < seq
    pos_q = jnp.where(q_frame, u_q + N_META, u_q - seq)
    qc = jnp.where(q_frame, 1 + (u_q >> 6), 0)
    u_k = start + col - BLK
    kc = 1 + (u_k >> 6)
    is_meta = col < BLK
    pos_k = jnp.where(is_meta, col, u_k + N_META)
    ok = (col < N_META) | ((col >= BLK) & (u_k < seq) & (kc <= qc) & (kc >= qc - WINDOW_CHUNKS))
    dist = jnp.abs(pos_q - pos_k).astype(F32)
    head_in_group = lax.broadcasted_iota(jnp.int32, (rows, 1), 0) >> 7

    outs = []
    for g in range(A_KV_HEADS):
        k_g = k_all[:, g * HEAD_DIM:(g + 1) * HEAD_DIM]
        v_g = v_all[:, g * HEAD_DIM:(g + 1) * HEAD_DIM]
        q_g = jnp.concatenate(
            [q[:, (g * A_GROUP + a) * HEAD_DIM:(g * A_GROUP + a + 1) * HEAD_DIM]
             for a in range(A_GROUP)], axis=0)
        slope = jnp.zeros((rows, 1), F32)
        sink = jnp.zeros((rows, 1), F32)
        for a in range(A_GROUP):
            h = g * A_GROUP + a
            slope = jnp.where(head_in_group == a, slopes_ref[h], slope)
            sink = jnp.where(head_in_group == a, sinks_ref[h], sink)
        s = _nt_dot(q_g, k_g) * SCALE
        s = jnp.where(ok, s - slope * dist, NEG)
        mx = jnp.maximum(jnp.max(s, axis=-1, keepdims=True), sink)
        e = jnp.exp(s - mx)
        denom = jnp.sum(e, axis=-1, keepdims=True) + jnp.exp(sink - mx)
        p = (e / denom).astype(BF16)
        o_g = jnp.dot(p, v_g, preferred_element_type=F32)
        for a in range(A_GROUP):
            outs.append(o_g[a * BLK:(a + 1) * BLK])
    o_ref[0] = jnp.concatenate(outs, axis=1).astype(o_ref.dtype)


def _attn_a(proj, slopes, sinks, seq):
    bsz, lp, _ = proj.shape
    kcol = A_Q // A_KV
    return pl.pallas_call(
        functools.partial(_attn_a_kernel, seq=seq),
        grid=(bsz, lp // BLK),
        in_specs=[
            pl.BlockSpec(memory_space=pltpu.SMEM),
            pl.BlockSpec(memory_space=pltpu.SMEM),
            pl.BlockSpec((1, BLK, A_Q), lambda b, m: (b, m, 0)),
            pl.BlockSpec((1, lp, A_KV), lambda b, m: (b, 0, kcol)),
            pl.BlockSpec((1, lp, A_KV), lambda b, m: (b, 0, kcol + 1)),
        ],
        out_specs=pl.BlockSpec((1, BLK, A_Q), lambda b, m: (b, m, 0)),
        out_shape=jax.ShapeDtypeStruct((bsz, lp, A_Q), BF16),
        compiler_params=_params(("arbitrary", "arbitrary")),
        name="attn_a",
    )(slopes, sinks, proj, proj, proj)


def _attn_b_kernel(q_ref, k_ref, vt_ref, o_ref, run_ref, acc_ref, *, seq):
    i = pl.program_id(2)
    n_ft = seq // TQ
    w = 2 * TQ

    qd = [_split_heads_rows(q_ref[0, :, g * BLK:(g + 1) * BLK] * SCALE)
          for g in range(NSTREAM)]
    run_ref[...] = jnp.zeros_like(run_ref)
    acc_ref[...] = jnp.zeros_like(acc_ref)

    def tri(tk):
        s_idx = lax.broadcasted_iota(jnp.int32, (tk, tk), 0)
        j_idx = lax.broadcasted_iota(jnp.int32, (tk, tk), 1)
        return jnp.where(j_idx > s_idx, 1.0, 0.0).astype(BF16)

    def tile(rows, tri_op, mask):
        streams = range(NSTREAM)
        zs = [_nt_dot(k_ref[0, rows, g * BLK:(g + 1) * BLK], qd[g]) for g in streams]
        sps, newers = [], []
        for g in streams:
            sp = jnp.maximum(zs[g], 0.0) + jnp.log(1.0 + jnp.exp(-jnp.abs(zs[g])))
            if mask is not None:
                sp = jnp.where(mask, sp, 0.0)
            newers.append(jnp.dot(tri_op, sp.astype(BF16), preferred_element_type=F32))
            sps.append(sp)
        pvs = []
        for g in streams:
            wgt = jnp.exp(zs[g] - (sps[g] + newers[g] + run_ref[g]))
            if mask is not None:
                wgt = jnp.where(mask, wgt, 0.0)
            pvs.append(jnp.dot(vt_ref[0, g * BLK:(g + 1) * BLK, rows], wgt.astype(BF16),
                               preferred_element_type=F32))
        for g in streams:
            acc_ref[g] += pvs[g]
            run_ref[g] += jnp.sum(sps[g], axis=0, keepdims=True)

    @pl.when(i < n_ft)
    def _():
        tri_tq = tri(TQ)
        r = lax.broadcasted_iota(jnp.int32, (TQ, w), 0)
        c = lax.broadcasted_iota(jnp.int32, (TQ, w), 1) & (TQ - 1)
        tile(pl.ds(pl.multiple_of(i * TQ, TQ), TQ), tri_tq, r < c)

        def body(jj, carry):
            tile(pl.ds(pl.multiple_of((i - 1 - jj) * TQ, TQ), TQ), tri_tq, None)
            return carry

        lax.fori_loop(0, i, body, 0)

    r = lax.broadcasted_iota(jnp.int32, (N_META, w), 0)
    c = lax.broadcasted_iota(jnp.int32, (N_META, w), 1) & (TQ - 1)
    tile(pl.ds(seq, N_META), tri(N_META), (i < n_ft) | (r < c))

    for g in range(NSTREAM):
        acc = acc_ref[g]
        o_t = jnp.concatenate([acc[:HEAD_DIM, :TQ], acc[HEAD_DIM:, TQ:]], axis=0)
        o_ref[0, :, g * BLK:(g + 1) * BLK] = o_t.T.astype(o_ref.dtype)


def _attn_b(proj, vt, seq):
    bsz, lp, _ = proj.shape
    gw = NSTREAM * BLK
    qcol = (A_Q + 2 * A_KV) // gw
    kcol = qcol + B_W // gw
    return pl.pallas_call(
        functools.partial(_attn_b_kernel, seq=seq),
        grid=(bsz, B_W // gw, pl.cdiv(lp, TQ)),
        in_specs=[
            pl.BlockSpec((1, TQ, gw), lambda b, p, i: (b, i, qcol + p)),
            pl.BlockSpec((1, lp, gw), lambda b, p, i: (b, 0, kcol + p)),
            pl.BlockSpec((1, gw, lp), lambda b, p, i: (b, p, 0)),
        ],
        out_specs=pl.BlockSpec((1, TQ, gw), lambda b, p, i: (b, i, p)),
        out_shape=jax.ShapeDtypeStruct((bsz, lp, B_W), BF16),
        scratch_shapes=[pltpu.VMEM((NSTREAM, 1, 2 * TQ), F32),
                        pltpu.VMEM((NSTREAM, BLK, 2 * TQ), F32)],
        compiler_params=_params(("arbitrary", "arbitrary", "arbitrary")),
        name="attn_b",
    )(proj, proj, vt)


def _attn_c_kernel(slopes_ref, lam_ref, g_ref, q_ref, k_ref, vt_ref, o_ref,
                   m_ref, acc_ref, *, seq, lambda_init):
    p = pl.program_id(1)
    i = pl.program_id(2)
    n_ft = seq // TQ
    w = 2 * TQ
    slope = [slopes_ref[p * NSTREAM + g] for g in range(NSTREAM)]
    qd = [_split_heads_rows(q_ref[0, :, g * BLK:(g + 1) * BLK] * SCALE)
          for g in range(NSTREAM)]
    meta = pl.ds(seq, N_META)

    def cols(g):
        return slice(g * BLK, (g + 1) * BLK)

    def vt_ones(g, rows):
        tk = rows.size
        return jnp.concatenate([vt_ref[0, cols(g), rows], jnp.ones((ACC_ROWS - BLK, tk), BF16)],
                               axis=0)

    def meta_scores(g):
        return _nt_dot(k_ref[0, meta, cols(g)], qd[g])

    rm = lax.broadcasted_iota(jnp.int32, (N_META, w), 0)
    cm = lax.broadcasted_iota(jnp.int32, (N_META, w), 1) & (TQ - 1)

    @pl.when(i == n_ft)
    def _():
        dist = jnp.abs(cm - rm).astype(F32)
        for g in range(NSTREAM):
            t = meta_scores(g) - slope[g] * dist
            m0 = jnp.max(t, axis=0, keepdims=True)
            e0 = jnp.exp(t - m0)
            m_ref[g] = m0
            acc_ref[g] = jnp.dot(vt_ones(g, meta), e0.astype(BF16), preferred_element_type=F32)

    @pl.when(i < n_ft)
    def _():
        r = lax.broadcasted_iota(jnp.int32, (TQ, w), 0)
        c = lax.broadcasted_iota(jnp.int32, (TQ, w), 1) & (TQ - 1)
        lane_k = lax.broadcasted_iota(jnp.int32, (TQ, BLK), 1)
        row_k = lax.broadcasted_iota(jnp.int32, (TQ, BLK), 0)
        key_off = jnp.where(lane_k < 3, row_k, 0).astype(F32).astype(BF16)
        lane_q = lax.broadcasted_iota(jnp.int32, (w, BLK), 1)
        qa = []
        for g in range(NSTREAM):
            sl = jnp.full((w, BLK), slope[g], F32)
            s_hi = sl.astype(BF16).astype(F32)
            s_mid = (sl - s_hi).astype(BF16).astype(F32)
            s_lo = (sl - s_hi) - s_mid
            sl3 = jnp.where(lane_q == 0, s_hi,
                            jnp.where(lane_q == 1, s_mid, jnp.where(lane_q == 2, s_lo, 0.0)))
            qa.append(jnp.concatenate([qd[g], sl3.astype(BF16)], axis=1))

        def scores(g, rows):
            return _nt_dot(jnp.concatenate([k_ref[0, rows, cols(g)], key_off], axis=1), qa[g])

        diag = pl.ds(pl.multiple_of(i * TQ, TQ), TQ)
        vis = (r >> 6) <= (c >> 6)
        ahead = 2.0 * jnp.maximum(r - c, 0).astype(F32)
        meta_off = (rm - (N_META + i * TQ)).astype(F32)
        s_d = [scores(g, diag) for g in range(NSTREAM)]
        s_m = [meta_scores(g) for g in range(NSTREAM)]
        e_d, e_m = [], []
        for g in range(NSTREAM):
            t_d = jnp.where(vis, s_d[g] - slope[g] * ahead, NEG)
            t_m = s_m[g] + slope[g] * meta_off
            m0 = jnp.maximum(jnp.max(t_d, axis=0, keepdims=True),
                             jnp.max(t_m, axis=0, keepdims=True))
            e_d.append(jnp.exp(t_d - m0).astype(BF16))
            e_m.append(jnp.exp(t_m - m0).astype(BF16))
            m_ref[g] = m0
        for g in range(NSTREAM):
            acc_ref[g] = (jnp.dot(vt_ones(g, diag), e_d[g], preferred_element_type=F32)
                          + jnp.dot(vt_ones(g, meta), e_m[g], preferred_element_type=F32))

        def body(j, carry):
            rows = pl.ds(pl.multiple_of(j * TQ, TQ), TQ)
            s = [scores(g, rows) for g in range(NSTREAM)]
            es, corrs = [], []
            for g in range(NSTREAM):
                cj = slope[g] * ((i - j) * TQ).astype(F32)
                m_old = m_ref[g]
                m_new = jnp.maximum(m_old, jnp.max(s[g], axis=0, keepdims=True) - cj)
                es.append(jnp.exp(s[g] - (m_new + cj)).astype(BF16))
                corrs.append(jnp.exp(m_old - m_new))
                m_ref[g] = m_new
            pvs = [jnp.dot(vt_ones(g, rows), es[g], preferred_element_type=F32)
                   for g in range(NSTREAM)]
            for g in range(NSTREAM):
                acc_ref[g] = corrs[g] * acc_ref[g] + pvs[g]
            return carry

        lax.fori_loop(0, i, body, 0)

    lv = lam_ref[...]
    t1 = jnp.sum(lv[0:1] * lv[1:2], axis=-1, keepdims=True)
    t2 = jnp.sum(lv[2:3] * lv[3:4], axis=-1, keepdims=True)
    lam = jnp.exp(t1) - jnp.exp(t2) + lambda_init

    for g in range(NSTREAM):
        acc = acc_ref[g]
        o_all = acc[:BLK] / acc[BLK:BLK + 1]
        o_t = o_all[:, :TQ] - lam * o_all[:, TQ:]
        ms = jnp.mean(o_t * o_t, axis=0, keepdims=True)
        o = (o_t * lax.rsqrt(ms + EPS)).T * g_ref[...] * (1.0 - lambda_init)
        o_ref[0, :, cols(g)] = o.astype(o_ref.dtype)


def _attn_c(proj, vt, slopes, lam_vecs, subln_g, lambda_init, seq):
    bsz, lp, _ = proj.shape
    gw = NSTREAM * BLK
    kcol = C_QK // gw
    return pl.pallas_call(
        functools.partial(_attn_c_kernel, seq=seq, lambda_init=lambda_init),
        grid=(bsz, C_QK // gw, pl.cdiv(lp, TQ)),
        in_specs=[
            pl.BlockSpec(memory_space=pltpu.SMEM),
            pl.BlockSpec((4, HEAD_DIM), lambda b, p, i: (0, 0)),
            pl.BlockSpec((1, 2 * HEAD_DIM), lambda b, p, i: (0, 0)),
            pl.BlockSpec((1, TQ, gw), lambda b, p, i: (b, i, p)),
            pl.BlockSpec((1, lp, gw), lambda b, p, i: (b, 0, kcol + p)),
            pl.BlockSpec((1, gw, lp), lambda b, p, i: (b, p, 0)),
        ],
        out_specs=pl.BlockSpec((1, TQ, gw), lambda b, p, i: (b, i, p)),
        out_shape=jax.ShapeDtypeStruct((bsz, lp, C_QK), BF16),
        scratch_shapes=[pltpu.VMEM((NSTREAM, 1, 2 * TQ), F32),
                        pltpu.VMEM((NSTREAM, ACC_ROWS, 2 * TQ), F32)],
        compiler_params=_params(("arbitrary", "arbitrary", "arbitrary")),
        name="attn_c",
    )(slopes, lam_vecs, subln_g.reshape(1, 2 * HEAD_DIM), proj, proj, vt)


def _alibi_slopes(n):
    return jnp.exp2(-8.0 * (jnp.arange(n, dtype=F32) + 1.0) / n)


def kernel(x, meta_tokens, ab_norm, w_in_ab, attn_sinks, w_out_ab, c_norm, w_in_c,
           diff_lambda, diff_subln, w_out_c, mlp_norm, w_mlp_in, w_mlp_out, final_norm):
    bsz, seq, d = x.shape
    lp = seq + BLK
    assert seq % TQ == 0 and d == D_MODEL
    t = bsz * lp
    depth = mlp_norm.shape[0]

    meta = jnp.broadcast_to(meta_tokens.astype(x.dtype)[None], (bsz, N_META, d))
    filler = jnp.zeros((bsz, BLK - N_META, d), x.dtype)
    h = jnp.concatenate([x, meta, filler], axis=1).reshape(t, d)

    tm = lp // 2

    for layer in range(depth):
        li = layer // 2
        if layer % 2 == 0:
            proj = _norm_matmul(h, ab_norm[li], w_in_ab, li, relu2=False, tm=tm, tn=512)
            proj = proj.reshape(bsz, lp, AB_IN)
            out_a = _attn_a(proj, _alibi_slopes(A_HEADS), attn_sinks[li].astype(F32), seq)
            vt_b = jnp.swapaxes(proj[:, :, AB_IN - B_W:], 1, 2)
            out_b = _attn_b(proj, vt_b, seq)
            h = _matmul_res([out_a.reshape(t, A_Q), out_b.reshape(t, B_W)], w_out_ab, li, h,
                            tm=tm, tn=1024, tk=A_Q + B_W)
        else:
            lambda_init = 0.8 - 0.6 * math.exp(-0.3 * layer)
            proj = _norm_matmul(h, c_norm[li], w_in_c, li, relu2=False, tm=tm, tn=1024)
            proj = proj.reshape(bsz, lp, 3 * C_QK)
            vt_c = jnp.swapaxes(proj[:, :, 2 * C_QK:], 1, 2)
            out_c = _attn_c(proj, vt_c, _alibi_slopes(C_HEADS), diff_lambda[li].astype(F32),
                            diff_subln[li].astype(F32), lambda_init, seq)
            h = _matmul_res([out_c.reshape(t, C_QK)], w_out_c, li, h, tm=tm, tn=1024, tk=2048)
        u = _norm_matmul(h, mlp_norm[layer], w_mlp_in, layer, relu2=True, tm=tm, tn=1024)
        h = _matmul_res([u], w_mlp_out, layer, h, tm=tm, tn=1024, tk=2048)

    return _final_norm(h.reshape(bsz, lp, d), final_norm, seq)
```

```python
import functools
import math

import jax
import jax.numpy as jnp
from jax import lax
from jax.experimental import pallas as pl
from jax.experimental.pallas import tpu as pltpu

D_MODEL = 2048
N_META = 16
CHUNK = 64
HEAD_DIM = 64
EPS = 1e-6
NEG = -1e30

A_HEADS = 16
A_KV_HEADS = 4
A_GROUP = A_HEADS // A_KV_HEADS
WINDOW_CHUNKS = 2
B_HEADS = 16
C_HEADS = D_MODEL // (2 * HEAD_DIM)
D_FF = 4 * D_MODEL

A_Q = A_HEADS * HEAD_DIM
A_KV = A_KV_HEADS * HEAD_DIM
B_W = B_HEADS * HEAD_DIM
AB_IN = A_Q + 2 * A_KV + 3 * B_W
C_QK = C_HEADS * 2 * HEAD_DIM

BLK = 128
TQ = 256
NSTREAM = 8
B_STREAMS = 4
ACC_ROWS = BLK + 16
SCALE = HEAD_DIM ** -0.5

VMEM_LIMIT = 56 * 1024 * 1024

F32 = jnp.float32
BF16 = jnp.bfloat16


def _params(sem, vmem=VMEM_LIMIT):
    return pltpu.CompilerParams(dimension_semantics=sem, vmem_limit_bytes=vmem)


def _nt_dot(a, b):
    return lax.dot_general(a, b, (((1,), (1,)), ((), ())), preferred_element_type=F32)


def _split_heads_rows(x):
    lane = lax.broadcasted_iota(jnp.int32, x.shape, 1)
    zero = jnp.zeros_like(x)
    return jnp.concatenate([jnp.where(lane < HEAD_DIM, x, zero),
                            jnp.where(lane >= HEAD_DIM, x, zero)], axis=0)


def _norm_matmul_kernel(x_ref, g_ref, w_ref, o_ref, hn_ref, *, relu2):
    @pl.when(pl.program_id(1) == 0)
    def _():
        x = x_ref[...]
        ms = jnp.mean(x * x, axis=-1, keepdims=True)
        hn_ref[...] = (x * lax.rsqrt(ms + EPS) * g_ref[...]).astype(hn_ref.dtype)

    acc = jnp.dot(hn_ref[...], w_ref[...].astype(BF16), preferred_element_type=F32)
    if relu2:
        acc = jnp.square(jnp.maximum(acc, 0.0))
    o_ref[...] = acc.astype(o_ref.dtype)


def _norm_matmul(x, g, w_stack, li, *, relu2, tm, tn):
    t, k = x.shape
    n = w_stack.shape[2]
    assert t % tm == 0 and n % tn == 0
    return pl.pallas_call(
        functools.partial(_norm_matmul_kernel, relu2=relu2),
        grid=(t // tm, n // tn),
        in_specs=[
            pl.BlockSpec((tm, k), lambda i, j: (i, 0)),
            pl.BlockSpec((1, k), lambda i, j: (0, 0)),
            pl.BlockSpec((None, k, tn), lambda i, j: (li, 0, j)),
        ],
        out_specs=pl.BlockSpec((tm, tn), lambda i, j: (i, j)),
        out_shape=jax.ShapeDtypeStruct((t, n), BF16),
        scratch_shapes=[pltpu.VMEM((tm, k), BF16)],
        compiler_params=_params(("arbitrary", "arbitrary")),
        name="norm_matmul_relu2" if relu2 else "norm_matmul",
    )(x, g.reshape(1, k), w_stack)


def _matmul_res_kernel(*refs, n_a):
    a_refs, (w_ref, r_ref, o_ref) = refs[:n_a], refs[n_a:]
    w = w_ref[...].astype(BF16)
    ka = w.shape[0] // n_a
    part = jnp.dot(a_refs[0][...], w[:ka], preferred_element_type=F32)
    for s in range(1, n_a):
        part += jnp.dot(a_refs[s][...], w[s * ka:(s + 1) * ka], preferred_element_type=F32)

    @pl.when(pl.program_id(2) == 0)
    def _():
        o_ref[...] = r_ref[...] + part

    @pl.when(pl.program_id(2) != 0)
    def _():
        o_ref[...] += part


def _matmul_res(a_parts, w_stack, li, res, *, tm, tn, tk):
    n_a = len(a_parts)
    t = a_parts[0].shape[0]
    _, k, n = w_stack.shape
    assert sum(a.shape[1] for a in a_parts) == k and (n_a == 1 or tk == k)
    assert t % tm == 0 and n % tn == 0 and k % tk == 0
    return pl.pallas_call(
        functools.partial(_matmul_res_kernel, n_a=n_a),
        grid=(t // tm, n // tn, k // tk),
        in_specs=[pl.BlockSpec((tm, tk // n_a), lambda i, j, kk: (i, kk)) for _ in a_parts] + [
            pl.BlockSpec((None, tk, tn), lambda i, j, kk: (li, kk, j)),
            pl.BlockSpec((tm, tn), lambda i, j, kk: (i, j)),
        ],
        out_specs=pl.BlockSpec((tm, tn), lambda i, j, kk: (i, j)),
        out_shape=jax.ShapeDtypeStruct((t, n), F32),
        compiler_params=_params(("arbitrary", "arbitrary", "arbitrary")),
        name="matmul_res",
    )(*a_parts, w_stack, res)


def _final_norm_kernel(x_ref, g_ref, o_ref):
    x = x_ref[0]
    ms = jnp.mean(x * x, axis=-1, keepdims=True)
    o_ref[0] = x * lax.rsqrt(ms + EPS) * g_ref[...]


def _final_norm(h3, g, seq):
    bsz, lp, d = h3.shape
    rows = 2 * BLK
    return pl.pallas_call(
        _final_norm_kernel,
        grid=(bsz, seq // rows),
        in_specs=[
            pl.BlockSpec((1, rows, d), lambda b, r: (b, r, 0)),
            pl.BlockSpec((1, d), lambda b, r: (0, 0)),
        ],
        out_specs=pl.BlockSpec((1, rows, d), lambda b, r: (b, r, 0)),
        out_shape=jax.ShapeDtypeStruct((bsz, seq, d), F32),
        compiler_params=_params(("arbitrary", "arbitrary")),
        name="final_norm",
    )(h3, g.reshape(1, d))


A_BAND = TQ + BLK
A_VROWS = HEAD_DIM + 16


def _attn_a_kernel(slopes_ref, sinks_ref, q_ref, k_ref, vt_ref, o_ref, *, seq):
    i = pl.program_id(1)
    lp = k_ref.shape[1]
    wl = A_GROUP * TQ
    start = pl.multiple_of(jnp.clip(i * TQ - BLK, 0, lp - A_BAND), BLK)
    band = pl.ds(start, A_BAND)
    meta = pl.ds(seq, N_META)

    def positions(nrows):
        r = lax.broadcasted_iota(jnp.int32, (nrows, TQ), 0)
        u_q = i * TQ + lax.broadcasted_iota(jnp.int32, (nrows, TQ), 1)
        q_frame = u_q < seq
        return r, jnp.where(q_frame, u_q + N_META, u_q - seq), jnp.where(q_frame, 1 + (u_q >> 6), 0)

    r, pos_q, qc = positions(A_BAND)
    u_k = start + r
    kc = 1 + (u_k >> 6)
    vis = (u_k < seq) & (kc <= qc) & (kc >= qc - WINDOW_CHUNKS)
    dist = jnp.abs(pos_q - (u_k + N_META)).astype(F32)
    rm, pos_qm, _ = positions(N_META)
    dist_m = jnp.abs(pos_qm - rm).astype(F32)
    vis4 = jnp.concatenate([vis] * A_GROUP, axis=1)
    head_lane = lax.broadcasted_iota(jnp.int32, (1, wl), 1) >> 8

    def per_head(ref, g):
        v = jnp.zeros((1, wl), F32)
        for a in range(A_GROUP):
            v = jnp.where(head_lane == a, ref[g * A_GROUP + a], v)
        return v

    def dup(kt, g):
        kg = kt[:, g * HEAD_DIM:(g + 1) * HEAD_DIM]
        return jnp.concatenate([kg, kg], axis=1)

    groups = range(A_KV_HEADS)
    k_band, k_meta = k_ref[0, band, :], k_ref[0, meta, :]
    qd = [jnp.concatenate(
        [_split_heads_rows(q_ref[0, :, (2 * g + p) * BLK:(2 * g + p + 1) * BLK] * SCALE)
         for p in range(2)], axis=0) for g in groups]
    s_b = [_nt_dot(dup(k_band, g), qd[g]) for g in groups]
    s_m = [_nt_dot(dup(k_meta, g), qd[g]) for g in groups]
    e_b, e_m, extra = [], [], []
    for g in groups:
        slope, sink = per_head(slopes_ref, g), per_head(sinks_ref, g)
        t_b = jnp.where(vis4, s_b[g] - slope * jnp.concatenate([dist] * A_GROUP, axis=1), NEG)
        t_m = s_m[g] - slope * jnp.concatenate([dist_m] * A_GROUP, axis=1)
        mx = jnp.maximum(jnp.maximum(jnp.max(t_b, axis=0, keepdims=True),
                                     jnp.max(t_m, axis=0, keepdims=True)), sink)
        e_b.append(jnp.exp(t_b - mx).astype(BF16))
        e_m.append(jnp.exp(t_m - mx).astype(BF16))
        extra.append(jnp.exp(sink - mx))

    def vt_ones(g, rows):
        return jnp.concatenate([vt_ref[0, g * HEAD_DIM:(g + 1) * HEAD_DIM, rows],
                                jnp.ones((A_VROWS - HEAD_DIM, rows.size), BF16)], axis=0)

    acc = [jnp.dot(vt_ones(g, band), e_b[g], preferred_element_type=F32)
           + jnp.dot(vt_ones(g, meta), e_m[g], preferred_element_type=F32) for g in groups]
    for g in groups:
        o_t = acc[g][:HEAD_DIM] / (acc[g][HEAD_DIM:HEAD_DIM + 1] + extra[g])
        for p in range(2):
            pair = jnp.concatenate([o_t[:, (2 * p) * TQ:(2 * p + 1) * TQ],
                                    o_t[:, (2 * p + 1) * TQ:(2 * p + 2) * TQ]], axis=0)
            o_ref[0, :, (2 * g + p) * BLK:(2 * g + p + 1) * BLK] = pair.T.astype(o_ref.dtype)


def _attn_a(proj, vt, slopes, sinks, seq):
    bsz, lp, _ = proj.shape
    kcol = A_Q // A_KV
    return pl.pallas_call(
        functools.partial(_attn_a_kernel, seq=seq),
        grid=(bsz, pl.cdiv(lp, TQ)),
        in_specs=[
            pl.BlockSpec(memory_space=pltpu.SMEM),
            pl.BlockSpec(memory_space=pltpu.SMEM),
            pl.BlockSpec((1, TQ, A_Q), lambda b, i: (b, i, 0)),
            pl.BlockSpec((1, lp, A_KV), lambda b, i: (b, 0, kcol)),
            pl.BlockSpec((1, A_KV, lp), lambda b, i: (b, 0, 0)),
        ],
        out_specs=pl.BlockSpec((1, TQ, A_Q), lambda b, i: (b, i, 0)),
        out_shape=jax.ShapeDtypeStruct((bsz, lp, A_Q), BF16),
        compiler_params=_params(("arbitrary", "arbitrary")),
        name="attn_a",
    )(slopes, sinks, proj, proj, vt)


def _attn_b_kernel(q_ref, k_ref, vt_ref, o_ref, run_ref, acc_ref, *, seq):
    i = pl.program_id(2)
    n_ft = seq // TQ
    w = 2 * TQ

    streams = range(B_STREAMS)
    qd = [_split_heads_rows(q_ref[0, :, g * BLK:(g + 1) * BLK] * SCALE)
          for g in streams]
    run_ref[...] = jnp.zeros_like(run_ref)
    acc_ref[...] = jnp.zeros_like(acc_ref)

    def tri(tk):
        s_idx = lax.broadcasted_iota(jnp.int32, (tk, tk), 0)
        j_idx = lax.broadcasted_iota(jnp.int32, (tk, tk), 1)
        return jnp.where(j_idx > s_idx, 1.0, 0.0).astype(BF16)

    def tile(rows, tri_op, mask):
        zs =[_nt_dot(k_ref[0, rows, g * BLK:(g + 1) * BLK], qd[g]) for g in streams]
        sps, newers = [], []
        for g in streams:
            sp = jnp.maximum(zs[g], 0.0) + jnp.log(1.0 + jnp.exp(-jnp.abs(zs[g])))
            if mask is not None:
                sp = jnp.where(mask, sp, 0.0)
            newers.append(jnp.dot(tri_op, sp.astype(BF16), preferred_element_type=F32))
            sps.append(sp)
        pvs = []
        for g in streams:
            wgt = jnp.exp(zs[g] - (sps[g] + newers[g] + run_ref[g]))
            if mask is not None:
                wgt = jnp.where(mask, wgt, 0.0)
            pvs.append(jnp.dot(vt_ref[0, g * BLK:(g + 1) * BLK, rows], wgt.astype(BF16),
                               preferred_element_type=F32))
        for g in streams:
            acc_ref[g] += pvs[g]
            run_ref[g] += jnp.sum(sps[g], axis=0, keepdims=True)

    @pl.when(i < n_ft)
    def _():
        tri_tq = tri(TQ)
        r = lax.broadcasted_iota(jnp.int32, (TQ, w), 0)
        c = lax.broadcasted_iota(jnp.int32, (TQ, w), 1) & (TQ - 1)
        tile(pl.ds(pl.multiple_of(i * TQ, TQ), TQ), tri_tq, r < c)

        def body(jj, carry):
            tile(pl.ds(pl.multiple_of((i - 1 - jj) * TQ, TQ), TQ), tri_tq, None)
            return carry

        lax.fori_loop(0, i, body, 0)

    r = lax.broadcasted_iota(jnp.int32, (N_META, w), 0)
    c = lax.broadcasted_iota(jnp.int32, (N_META, w), 1) & (TQ - 1)
    tile(pl.ds(seq, N_META), tri(N_META), (i < n_ft) | (r < c))

    for g in streams:
        acc = acc_ref[g]
        o_t = jnp.concatenate([acc[:HEAD_DIM, :TQ], acc[HEAD_DIM:, TQ:]], axis=0)
        o_ref[0, :, g * BLK:(g + 1) * BLK] = o_t.T.astype(o_ref.dtype)


def _attn_b(proj, vt, seq):
    bsz, lp, _ = proj.shape
    gw = B_STREAMS * BLK
    assert (A_Q + 2 * A_KV) % gw == 0 and B_W % gw == 0
    qcol = (A_Q + 2 * A_KV) // gw
    kcol = qcol + B_W // gw
    return pl.pallas_call(
        functools.partial(_attn_b_kernel, seq=seq),
        grid=(bsz, B_W // gw, pl.cdiv(lp, TQ)),
        in_specs=[
            pl.BlockSpec((1, TQ, gw), lambda b, p, i: (b, i, qcol + p)),
            pl.BlockSpec((1, lp, gw), lambda b, p, i: (b, 0, kcol + p)),
            pl.BlockSpec((1, gw, lp), lambda b, p, i: (b, p, 0)),
        ],
        out_specs=pl.BlockSpec((1, TQ, gw), lambda b, p, i: (b, i, p)),
        out_shape=jax.ShapeDtypeStruct((bsz, lp, B_W), BF16),
        scratch_shapes=[pltpu.VMEM((B_STREAMS, 1, 2 * TQ), F32),
                        pltpu.VMEM((B_STREAMS, BLK, 2 * TQ), F32)],
        compiler_params=_params(("arbitrary", "arbitrary", "arbitrary")),
        name="attn_b",
    )(proj, proj, vt)


def _attn_c_kernel(slopes_ref, lam_ref, g_ref, q_ref, k_ref, vt_ref, o_ref,
                   m_ref, acc_ref, *, seq, lambda_init):
    p = pl.program_id(1)
    i = pl.program_id(2)
    n_ft = seq // TQ
    w = 2 * TQ
    slope = [slopes_ref[p * NSTREAM + g] for g in range(NSTREAM)]
    qd = [_split_heads_rows(q_ref[0, :, g * BLK:(g + 1) * BLK] * SCALE)
          for g in range(NSTREAM)]
    meta = pl.ds(seq, N_META)

    def cols(g):
        return slice(g * BLK, (g + 1) * BLK)

    def vt_ones(g, rows):
        tk = rows.size
        return jnp.concatenate([vt_ref[0, cols(g), rows], jnp.ones((ACC_ROWS - BLK, tk), BF16)],
                               axis=0)

    def meta_scores(g):
        return _nt_dot(k_ref[0, meta, cols(g)], qd[g])

    rm = lax.broadcasted_iota(jnp.int32, (N_META, w), 0)
    cm = lax.broadcasted_iota(jnp.int32, (N_META, w), 1) & (TQ - 1)

    @pl.when(i == n_ft)
    def _():
        dist = jnp.abs(cm - rm).astype(F32)
        for g in range(NSTREAM):
            t = meta_scores(g) - slope[g] * dist
            m0 = jnp.max(t, axis=0, keepdims=True)
            e0 = jnp.exp(t - m0)
            m_ref[g] = m0
            acc_ref[g] = jnp.dot(vt_ones(g, meta), e0.astype(BF16), preferred_element_type=F32)

    @pl.when(i < n_ft)
    def _():
        r = lax.broadcasted_iota(jnp.int32, (TQ, w), 0)
        c = lax.broadcasted_iota(jnp.int32, (TQ, w), 1) & (TQ - 1)
        lane_k = lax.broadcasted_iota(jnp.int32, (TQ, BLK), 1)
        row_k = lax.broadcasted_iota(jnp.int32, (TQ, BLK), 0)
        key_off = jnp.where(lane_k < 3, row_k, 0).astype(F32).astype(BF16)
        lane_q = lax.broadcasted_iota(jnp.int32, (w, BLK), 1)
        qa = []
        for g in range(NSTREAM):
            sl = jnp.full((w, BLK), slope[g], F32)
            s_hi = sl.astype(BF16).astype(F32)
            s_mid = (sl - s_hi).astype(BF16).astype(F32)
            s_lo = (sl - s_hi) - s_mid
            sl3 = jnp.where(lane_q == 0, s_hi,
                            jnp.where(lane_q == 1, s_mid, jnp.where(lane_q == 2, s_lo, 0.0)))
            qa.append(jnp.concatenate([qd[g], sl3.astype(BF16)], axis=1))

        def scores(g, rows):
            return _nt_dot(jnp.concatenate([k_ref[0, rows, cols(g)], key_off], axis=1), qa[g])

        diag = pl.ds(pl.multiple_of(i * TQ, TQ), TQ)
        vis = (r >> 6) <= (c >> 6)
        ahead = 2.0 * jnp.maximum(r - c, 0).astype(F32)
        meta_off = (rm - (N_META + i * TQ)).astype(F32)
        s_d = [scores(g, diag) for g in range(NSTREAM)]
        s_m = [meta_scores(g) for g in range(NSTREAM)]
        e_d, e_m = [], []
        for g in range(NSTREAM):
            t_d = jnp.where(vis, s_d[g] - slope[g] * ahead, NEG)
            t_m = s_m[g] + slope[g] * meta_off
            m0 = jnp.maximum(jnp.max(t_d, axis=0, keepdims=True),
                             jnp.max(t_m, axis=0, keepdims=True))
            e_d.append(jnp.exp(t_d - m0).astype(BF16))
            e_m.append(jnp.exp(t_m - m0).astype(BF16))
            m_ref[g] = m0
        for g in range(NSTREAM):
            acc_ref[g] = (jnp.dot(vt_ones(g, diag), e_d[g], preferred_element_type=F32)
                          + jnp.dot(vt_ones(g, meta), e_m[g], preferred_element_type=F32))

        def body(j, carry):
            rows = pl.ds(pl.multiple_of(j * TQ, TQ), TQ)
            s = [scores(g, rows) for g in range(NSTREAM)]
            es, corrs = [], []
            for g in range(NSTREAM):
                cj = slope[g] * ((i - j) * TQ).astype(F32)
                m_old = m_ref[g]
                m_new = jnp.maximum(m_old, jnp.max(s[g], axis=0, keepdims=True) - cj)
                es.append(jnp.exp(s[g] - (m_new + cj)).astype(BF16))
                corrs.append(jnp.exp(m_old - m_new))
                m_ref[g] = m_new
            pvs = [jnp.dot(vt_ones(g, rows), es[g], preferred_element_type=F32)
                   for g in range(NSTREAM)]
            for g in range(NSTREAM):
                acc_ref[g] = corrs[g] * acc_ref[g] + pvs[g]
            return carry

        lax.fori_loop(0, i, body, 0)

    lv = lam_ref[...]
    t1 = jnp.sum(lv[0:1] * lv[1:2], axis=-1, keepdims=True)
    t2 = jnp.sum(lv[2:3] * lv[3:4], axis=-1, keepdims=True)
    lam = jnp.exp(t1) - jnp.exp(t2) + lambda_init

    for g in range(NSTREAM):
        acc = acc_ref[g]
        o_all = acc[:BLK] / acc[BLK:BLK + 1]
        o_t = o_all[:, :TQ] - lam * o_all[:, TQ:]
        ms = jnp.mean(o_t * o_t, axis=0, keepdims=True)
        o = (o_t * lax.rsqrt(ms + EPS)).T * g_ref[...] * (1.0 - lambda_init)
        o_ref[0, :, cols(g)] = o.astype(o_ref.dtype)


def _attn_c(proj, vt, slopes, lam_vecs, subln_g, lambda_init, seq):
    bsz, lp, _ = proj.shape
    gw = NSTREAM * BLK
    kcol = C_QK // gw
    return pl.pallas_call(
        functools.partial(_attn_c_kernel, seq=seq, lambda_init=lambda_init),
        grid=(bsz, C_QK // gw, pl.cdiv(lp, TQ)),
        in_specs=[
            pl.BlockSpec(memory_space=pltpu.SMEM),
            pl.BlockSpec((4, HEAD_DIM), lambda b, p, i: (0, 0)),
            pl.BlockSpec((1, 2 * HEAD_DIM), lambda b, p, i: (0, 0)),
            pl.BlockSpec((1, TQ, gw), lambda b, p, i: (b, i, p)),
            pl.BlockSpec((1, lp, gw), lambda b, p, i: (b, 0, kcol + p)),
            pl.BlockSpec((1, gw, lp), lambda b, p, i: (b, p, 0)),
        ],
        out_specs=pl.BlockSpec((1, TQ, gw), lambda b, p, i: (b, i, p)),
        out_shape=jax.ShapeDtypeStruct((bsz, lp, C_QK), BF16),
        scratch_shapes=[pltpu.VMEM((NSTREAM, 1, 2 * TQ), F32),
                        pltpu.VMEM((NSTREAM, ACC_ROWS, 2 * TQ), F32)],
        compiler_params=_params(("arbitrary", "arbitrary", "arbitrary")),
        name="attn_c",
    )(slopes, lam_vecs, subln_g.reshape(1, 2 * HEAD_DIM), proj, proj, vt)


def _alibi_slopes(n):
    return jnp.exp2(-8.0 * (jnp.arange(n, dtype=F32) + 1.0) / n)


def kernel(x, meta_tokens, ab_norm, w_in_ab, attn_sinks, w_out_ab, c_norm, w_in_c,
           diff_lambda, diff_subln, w_out_c, mlp_norm, w_mlp_in, w_mlp_out, final_norm):
    bsz, seq, d = x.shape
    lp = seq + BLK
    assert seq % TQ == 0 and d == D_MODEL
    t = bsz * lp
    depth = mlp_norm.shape[0]

    meta = jnp.broadcast_to(meta_tokens.astype(x.dtype)[None], (bsz, N_META, d))
    filler = jnp.zeros((bsz, BLK - N_META, d), x.dtype)
    h = jnp.concatenate([x, meta, filler], axis=1).reshape(t, d)

    tm = lp // 2

    for layer in range(depth):
        li = layer // 2
        if layer % 2 == 0:
            proj = _norm_matmul(h, ab_norm[li], w_in_ab, li, relu2=False, tm=tm, tn=512)
            proj = proj.reshape(bsz, lp, AB_IN)
            vt_a = jnp.swapaxes(proj[:, :, A_Q + A_KV:A_Q + 2 * A_KV], 1, 2)
            out_a = _attn_a(proj, vt_a, _alibi_slopes(A_HEADS), attn_sinks[li].astype(F32), seq)
            vt_b = jnp.swapaxes(proj[:, :, AB_IN - B_W:], 1, 2)
            out_b = _attn_b(proj, vt_b, seq)
            h = _matmul_res([out_a.reshape(t, A_Q), out_b.reshape(t, B_W)], w_out_ab, li, h,
                            tm=tm, tn=1024, tk=A_Q + B_W)
        else:
            lambda_init = 0.8 - 0.6 * math.exp(-0.3 * layer)
            proj = _norm_matmul(h, c_norm[li], w_in_c, li, relu2=False, tm=tm, tn=1024)
            proj = proj.reshape(bsz, lp, 3 * C_QK)
            vt_c = jnp.swapaxes(proj[:, :, 2 * C_QK:], 1, 2)
            out_c = _attn_c(proj, vt_c, _alibi_slopes(C_HEADS), diff_lambda[li].astype(F32),
                            diff_subln[li].astype(F32), lambda_init, seq)
            h = _matmul_res([out_c.reshape(t, C_QK)], w_out_c, li, h, tm=tm, tn=1024, tk=2048)
        u = _norm_matmul(h, mlp_norm[layer], w_mlp_in, layer, relu2=True, tm=tm, tn=1024)
        h = _matmul_res([u], w_mlp_out, layer, h, tm=tm, tn=1024, tk=2048)

    return _final_norm(h.reshape(bsz, lp, d), final_norm, seq)
```

```python
import functools
import math

import jax
import jax.numpy as jnp
from jax import lax
from jax.experimental import pallas as pl
from jax.experimental.pallas import tpu as pltpu

D_MODEL = 2048
N_META = 16
CHUNK = 64
HEAD_DIM = 64
EPS = 1e-6
NEG = -1e30

A_HEADS = 16
A_KV_HEADS = 4
A_GROUP = A_HEADS // A_KV_HEADS
WINDOW_CHUNKS = 2
B_HEADS = 16
C_HEADS = D_MODEL // (2 * HEAD_DIM)
D_FF = 4 * D_MODEL

A_Q = A_HEADS * HEAD_DIM
A_KV = A_KV_HEADS * HEAD_DIM
B_W = B_HEADS * HEAD_DIM
AB_IN = A_Q + 2 * A_KV + 3 * B_W
C_QK = C_HEADS * 2 * HEAD_DIM

BLK = 128
TQ = 256
NSTREAM = 8
B_STREAMS = 4
ACC_ROWS = BLK + 16
SCALE = HEAD_DIM ** -0.5

VMEM_LIMIT = 56 * 1024 * 1024

F32 = jnp.float32
BF16 = jnp.bfloat16


def _params(sem, vmem=VMEM_LIMIT):
    return pltpu.CompilerParams(dimension_semantics=sem, vmem_limit_bytes=vmem)


def _nt_dot(a, b):
    return lax.dot_general(a, b, (((1,), (1,)), ((), ())), preferred_element_type=F32)


def _split_heads_rows(x):
    lane = lax.broadcasted_iota(jnp.int32, x.shape, 1)
    zero = jnp.zeros_like(x)
    return jnp.concatenate([jnp.where(lane < HEAD_DIM, x, zero),
                            jnp.where(lane >= HEAD_DIM, x, zero)], axis=0)


def _norm_matmul_kernel(x_ref, g_ref, w_ref, o_ref, hn_ref, *, relu2):
    @pl.when(pl.program_id(1) == 0)
    def _():
        x = x_ref[...]
        ms = jnp.mean(x * x, axis=-1, keepdims=True)
        hn_ref[...] = (x * lax.rsqrt(ms + EPS) * g_ref[...]).astype(hn_ref.dtype)

    acc = jnp.dot(hn_ref[...], w_ref[...].astype(BF16), preferred_element_type=F32)
    if relu2:
        acc = jnp.square(jnp.maximum(acc, 0.0))
    o_ref[...] = acc.astype(o_ref.dtype)


def _norm_matmul(x, g, w_stack, li, *, relu2, tm, tn, x_buffers=2):
    t, k = x.shape
    n = w_stack.shape[2]
    assert t % tm == 0 and n % tn == 0
    return pl.pallas_call(
        functools.partial(_norm_matmul_kernel, relu2=relu2),
        grid=(t // tm, n // tn),
        in_specs=[
            pl.BlockSpec((tm, k), lambda i, j: (i, 0), pipeline_mode=pl.Buffered(x_buffers)),
            pl.BlockSpec((1, k), lambda i, j: (0, 0)),
            pl.BlockSpec((None, k, tn), lambda i, j: (li, 0, j)),
        ],
        out_specs=pl.BlockSpec((tm, tn), lambda i, j: (i, j)),
        out_shape=jax.ShapeDtypeStruct((t, n), BF16),
        scratch_shapes=[pltpu.VMEM((tm, k), BF16)],
        compiler_params=_params(("arbitrary", "arbitrary")),
        name="norm_matmul_relu2" if relu2 else "norm_matmul",
    )(x, g.reshape(1, k), w_stack)


def _matmul_res_kernel(*refs, n_a):
    a_refs, (w_ref, r_ref, o_ref) = refs[:n_a], refs[n_a:]
    w = w_ref[...].astype(BF16)
    ka = w.shape[0] // n_a
    part = jnp.dot(a_refs[0][...], w[:ka], preferred_element_type=F32)
    for s in range(1, n_a):
        part += jnp.dot(a_refs[s][...], w[s * ka:(s + 1) * ka], preferred_element_type=F32)

    @pl.when(pl.program_id(2) == 0)
    def _():
        o_ref[...] = r_ref[...] + part

    @pl.when(pl.program_id(2) != 0)
    def _():
        o_ref[...] += part


def _matmul_res(a_parts, w_stack, li, res, *, tm, tn, tk):
    n_a = len(a_parts)
    t = a_parts[0].shape[0]
    _, k, n = w_stack.shape
    assert sum(a.shape[1] for a in a_parts) == k and (n_a == 1 or tk == k)
    assert t % tm == 0 and n % tn == 0 and k % tk == 0
    return pl.pallas_call(
        functools.partial(_matmul_res_kernel, n_a=n_a),
        grid=(t // tm, n // tn, k // tk),
        in_specs=[pl.BlockSpec((tm, tk // n_a), lambda i, j, kk: (i, kk)) for _ in a_parts] + [
            pl.BlockSpec((None, tk, tn), lambda i, j, kk: (li, kk, j)),
            pl.BlockSpec((tm, tn), lambda i, j, kk: (i, j)),
        ],
        out_specs=pl.BlockSpec((tm, tn), lambda i, j, kk: (i, j)),
        out_shape=jax.ShapeDtypeStruct((t, n), F32),
        compiler_params=_params(("arbitrary", "arbitrary", "arbitrary")),
        name="matmul_res",
    )(*a_parts, w_stack, res)


def _final_norm_kernel(x_ref, g_ref, o_ref):
    x = x_ref[0]
    ms = jnp.mean(x * x, axis=-1, keepdims=True)
    o_ref[0] = x * lax.rsqrt(ms + EPS) * g_ref[...]


def _final_norm(h3, g, seq):
    bsz, lp, d = h3.shape
    rows = 2 * BLK
    return pl.pallas_call(
        _final_norm_kernel,
        grid=(bsz, seq // rows),
        in_specs=[
            pl.BlockSpec((1, rows, d), lambda b, r: (b, r, 0)),
            pl.BlockSpec((1, d), lambda b, r: (0, 0)),
        ],
        out_specs=pl.BlockSpec((1, rows, d), lambda b, r: (b, r, 0)),
        out_shape=jax.ShapeDtypeStruct((bsz, seq, d), F32),
        compiler_params=_params(("arbitrary", "arbitrary")),
        name="final_norm",
    )(h3, g.reshape(1, d))


A_BAND = TQ + BLK
A_VROWS = HEAD_DIM + 16


def _attn_a_kernel(slopes_ref, sinks_ref, q_ref, k_ref, vt_ref, o_ref, *, seq):
    i = pl.program_id(1)
    lp = k_ref.shape[1]
    wl = A_GROUP * TQ
    start = pl.multiple_of(jnp.clip(i * TQ - BLK, 0, lp - A_BAND), BLK)
    band = pl.ds(start, A_BAND)
    meta = pl.ds(seq, N_META)

    def positions(nrows):
        r = lax.broadcasted_iota(jnp.int32, (nrows, TQ), 0)
        u_q = i * TQ + lax.broadcasted_iota(jnp.int32, (nrows, TQ), 1)
        q_frame = u_q < seq
        return r, jnp.where(q_frame, u_q + N_META, u_q - seq), jnp.where(q_frame, 1 + (u_q >> 6), 0)

    r, pos_q, qc = positions(A_BAND)
    u_k = start + r
    kc = 1 + (u_k >> 6)
    vis = (u_k < seq) & (kc <= qc) & (kc >= qc - WINDOW_CHUNKS)
    dist = jnp.abs(pos_q - (u_k + N_META)).astype(F32)
    rm, pos_qm, _ = positions(N_META)
    dist_m = jnp.abs(pos_qm - rm).astype(F32)
    vis4 = jnp.concatenate([vis] * A_GROUP, axis=1)
    head_lane = lax.broadcasted_iota(jnp.int32, (1, wl), 1) >> 8

    def per_head(ref, g):
        v = jnp.zeros((1, wl), F32)
        for a in range(A_GROUP):
            v = jnp.where(head_lane == a, ref[g * A_GROUP + a], v)
        return v

    def dup(kt, g):
        kg = kt[:, g * HEAD_DIM:(g + 1) * HEAD_DIM]
        return jnp.concatenate([kg, kg], axis=1)

    groups = range(A_KV_HEADS)
    k_band, k_meta = k_ref[0, band, :], k_ref[0, meta, :]
    qd = [jnp.concatenate(
        [_split_heads_rows(q_ref[0, :, (2 * g + p) * BLK:(2 * g + p + 1) * BLK] * SCALE)
         for p in range(2)], axis=0) for g in groups]
    s_b = [_nt_dot(dup(k_band, g), qd[g]) for g in groups]
    s_m = [_nt_dot(dup(k_meta, g), qd[g]) for g in groups]
    e_b, e_m, extra = [], [], []
    for g in groups:
        slope, sink = per_head(slopes_ref, g), per_head(sinks_ref, g)
        t_b = jnp.where(vis4, s_b[g] - slope * jnp.concatenate([dist] * A_GROUP, axis=1), NEG)
        t_m = s_m[g] - slope * jnp.concatenate([dist_m] * A_GROUP, axis=1)
        mx = jnp.maximum(jnp.maximum(jnp.max(t_b, axis=0, keepdims=True),
                                     jnp.max(t_m, axis=0, keepdims=True)), sink)
        e_b.append(jnp.exp(t_b - mx).astype(BF16))
        e_m.append(jnp.exp(t_m - mx).astype(BF16))
        extra.append(jnp.exp(sink - mx))

    def vt_ones(g, rows):
        return jnp.concatenate([vt_ref[0, g * HEAD_DIM:(g + 1) * HEAD_DIM, rows],
                                jnp.ones((A_VROWS - HEAD_DIM, rows.size), BF16)], axis=0)

    acc = [jnp.dot(vt_ones(g, band), e_b[g], preferred_element_type=F32)
           + jnp.dot(vt_ones(g, meta), e_m[g], preferred_element_type=F32) for g in groups]
    for g in groups:
        o_t = acc[g][:HEAD_DIM] / (acc[g][HEAD_DIM:HEAD_DIM + 1] + extra[g])
        for p in range(2):
            pair = jnp.concatenate([o_t[:, (2 * p) * TQ:(2 * p + 1) * TQ],
                                    o_t[:, (2 * p + 1) * TQ:(2 * p + 2) * TQ]], axis=0)
            o_ref[0, :, (2 * g + p) * BLK:(2 * g + p + 1) * BLK] = pair.T.astype(o_ref.dtype)


def _attn_a(proj, vt, slopes, sinks, seq):
    bsz, lp, _ = proj.shape
    kcol = A_Q // A_KV
    return pl.pallas_call(
        functools.partial(_attn_a_kernel, seq=seq),
        grid=(bsz, pl.cdiv(lp, TQ)),
        in_specs=[
            pl.BlockSpec(memory_space=pltpu.SMEM),
            pl.BlockSpec(memory_space=pltpu.SMEM),
            pl.BlockSpec((1, TQ, A_Q), lambda b, i: (b, i, 0)),
            pl.BlockSpec((1, lp, A_KV), lambda b, i: (b, 0, kcol)),
            pl.BlockSpec((1, A_KV, lp), lambda b, i: (b, 0, 0)),
        ],
        out_specs=pl.BlockSpec((1, TQ, A_Q), lambda b, i: (b, i, 0)),
        out_shape=jax.ShapeDtypeStruct((bsz, lp, A_Q), BF16),
        compiler_params=_params(("arbitrary", "arbitrary")),
        name="attn_a",
    )(slopes, sinks, proj, proj, vt)


def _attn_b_kernel(q_ref, k_ref, vt_ref, o_ref, run_ref, acc_ref, *, seq):
    i = pl.program_id(2)
    n_ft = seq // TQ
    w = 2 * TQ

    streams = range(B_STREAMS)
    qd = [_split_heads_rows(q_ref[0, :, g * BLK:(g + 1) * BLK] * SCALE)
          for g in streams]
    run_ref[...] = jnp.zeros_like(run_ref)
    acc_ref[...] = jnp.zeros_like(acc_ref)

    def tri(tk):
        s_idx = lax.broadcasted_iota(jnp.int32, (tk + 16, tk), 0)
        j_idx = lax.broadcasted_iota(jnp.int32, (tk + 16, tk), 1)
        return jnp.where((j_idx > s_idx) | (s_idx >= tk), 1.0, 0.0).astype(BF16)

    def tile(rows, tri_op, mask):
        zs =[_nt_dot(k_ref[0, rows, g * BLK:(g + 1) * BLK], qd[g]) for g in streams]
        tk = rows.size
        sps, newers = [], []
        for g in streams:
            sp = jnp.maximum(zs[g], 0.0) + jnp.log(1.0 + jnp.exp(-jnp.abs(zs[g])))
            if mask is not None:
                sp = jnp.where(mask, sp, 0.0)
            newers.append(jnp.dot(tri_op, sp.astype(BF16), preferred_element_type=F32))
            sps.append(sp)
        pvs = []
        for g in streams:
            wgt = jnp.exp(zs[g] - (sps[g] + newers[g][:tk] + run_ref[g]))
            if mask is not None:
                wgt = jnp.where(mask, wgt, 0.0)
            pvs.append(jnp.dot(vt_ref[0, g * BLK:(g + 1) * BLK, rows], wgt.astype(BF16),
                               preferred_element_type=F32))
        for g in streams:
            acc_ref[g] += pvs[g]
            run_ref[g] += newers[g][tk:tk + 1]

    @pl.when(i < n_ft)
    def _():
        tri_tq = tri(TQ)
        r = lax.broadcasted_iota(jnp.int32, (TQ, w), 0)
        c = lax.broadcasted_iota(jnp.int32, (TQ, w), 1) & (TQ - 1)
        tile(pl.ds(pl.multiple_of(i * TQ, TQ), TQ), tri_tq, r < c)

        def body(jj, carry):
            tile(pl.ds(pl.multiple_of((i - 1 - jj) * TQ, TQ), TQ), tri_tq, None)
            return carry

        lax.fori_loop(0, i, body, 0)

    r = lax.broadcasted_iota(jnp.int32, (N_META, w), 0)
    c = lax.broadcasted_iota(jnp.int32, (N_META, w), 1) & (TQ - 1)
    tile(pl.ds(seq, N_META), tri(N_META), (i < n_ft) | (r < c))

    for g in streams:
        acc = acc_ref[g]
        o_t = jnp.concatenate([acc[:HEAD_DIM, :TQ], acc[HEAD_DIM:, TQ:]], axis=0)
        o_ref[0, :, g * BLK:(g + 1) * BLK] = o_t.T.astype(o_ref.dtype)


def _attn_b(proj, vt, seq):
    bsz, lp, _ = proj.shape
    gw = B_STREAMS * BLK
    assert (A_Q + 2 * A_KV) % gw == 0 and B_W % gw == 0
    qcol = (A_Q + 2 * A_KV) // gw
    kcol = qcol + B_W // gw
    return pl.pallas_call(
        functools.partial(_attn_b_kernel, seq=seq),
        grid=(bsz, B_W // gw, pl.cdiv(lp, TQ)),
        in_specs=[
            pl.BlockSpec((1, TQ, gw), lambda b, p, i: (b, i, qcol + p)),
            pl.BlockSpec((1, lp, gw), lambda b, p, i: (b, 0, kcol + p)),
            pl.BlockSpec((1, gw, lp), lambda b, p, i: (b, p, 0)),
        ],
        out_specs=pl.BlockSpec((1, TQ, gw), lambda b, p, i: (b, i, p)),
        out_shape=jax.ShapeDtypeStruct((bsz, lp, B_W), BF16),
        scratch_shapes=[pltpu.VMEM((B_STREAMS, 1, 2 * TQ), F32),
                        pltpu.VMEM((B_STREAMS, BLK, 2 * TQ), F32)],
        compiler_params=_params(("arbitrary", "arbitrary", "arbitrary")),
        name="attn_b",
    )(proj, proj, vt)


def _attn_c_kernel(slopes_ref, lam_ref, g_ref, q_ref, k_ref, vt_ref, o_ref,
                   m_ref, acc_ref, *, seq, lambda_init):
    p = pl.program_id(1)
    i = pl.program_id(2)
    n_ft = seq // TQ
    w = 2 * TQ
    slope = [slopes_ref[p * NSTREAM + g] for g in range(NSTREAM)]
    qd = [_split_heads_rows(q_ref[0, :, g * BLK:(g + 1) * BLK] * SCALE)
          for g in range(NSTREAM)]
    meta = pl.ds(seq, N_META)

    def cols(g):
        return slice(g * BLK, (g + 1) * BLK)

    def vt_ones(g, rows):
        tk = rows.size
        return jnp.concatenate([vt_ref[0, cols(g), rows], jnp.ones((ACC_ROWS - BLK, tk), BF16)],
                               axis=0)

    def meta_scores(g):
        return _nt_dot(k_ref[0, meta, cols(g)], qd[g])

    rm = lax.broadcasted_iota(jnp.int32, (N_META, w), 0)
    cm = lax.broadcasted_iota(jnp.int32, (N_META, w), 1) & (TQ - 1)

    @pl.when(i == n_ft)
    def _():
        dist = jnp.abs(cm - rm).astype(F32)
        for g in range(NSTREAM):
            t = meta_scores(g) - slope[g] * dist
            m0 = jnp.max(t, axis=0, keepdims=True)
            e0 = jnp.exp(t - m0)
            m_ref[g] = m0
            acc_ref[g] = jnp.dot(vt_ones(g, meta), e0.astype(BF16), preferred_element_type=F32)

    @pl.when(i < n_ft)
    def _():
        r = lax.broadcasted_iota(jnp.int32, (TQ, w), 0)
        c = lax.broadcasted_iota(jnp.int32, (TQ, w), 1) & (TQ - 1)
        lane_k = lax.broadcasted_iota(jnp.int32, (TQ, BLK), 1)
        row_k = lax.broadcasted_iota(jnp.int32, (TQ, BLK), 0)
        key_off = jnp.where(lane_k < 3, row_k, 0).astype(F32).astype(BF16)
        lane_q = lax.broadcasted_iota(jnp.int32, (w, BLK), 1)
        qa = []
        for g in range(NSTREAM):
            sl = jnp.full((w, BLK), slope[g], F32)
            s_hi = sl.astype(BF16).astype(F32)
            s_mid = (sl - s_hi).astype(BF16).astype(F32)
            s_lo = (sl - s_hi) - s_mid
            sl3 = jnp.where(lane_q == 0, s_hi,
                            jnp.where(lane_q == 1, s_mid, jnp.where(lane_q == 2, s_lo, 0.0)))
            qa.append(jnp.concatenate([qd[g], sl3.astype(BF16)], axis=1))

        def scores(g, rows):
            return _nt_dot(jnp.concatenate([k_ref[0, rows, cols(g)], key_off], axis=1), qa[g])

        diag = pl.ds(pl.multiple_of(i * TQ, TQ), TQ)
        vis = (r >> 6) <= (c >> 6)
        ahead = 2.0 * jnp.maximum(r - c, 0).astype(F32)
        meta_off = (rm - (N_META + i * TQ)).astype(F32)
        s_d = [scores(g, diag) for g in range(NSTREAM)]
        s_m = [meta_scores(g) for g in range(NSTREAM)]
        e_d, e_m = [], []
        for g in range(NSTREAM):
            t_d = jnp.where(vis, s_d[g] - slope[g] * ahead, NEG)
            t_m = s_m[g] + slope[g] * meta_off
            m0 = jnp.maximum(jnp.max(t_d, axis=0, keepdims=True),
                             jnp.max(t_m, axis=0, keepdims=True))
            e_d.append(jnp.exp(t_d - m0).astype(BF16))
            e_m.append(jnp.exp(t_m - m0).astype(BF16))
            m_ref[g] = m0
        for g in range(NSTREAM):
            acc_ref[g] = (jnp.dot(vt_ones(g, diag), e_d[g], preferred_element_type=F32)
                          + jnp.dot(vt_ones(g, meta), e_m[g], preferred_element_type=F32))

        def body(j, carry):
            rows = pl.ds(pl.multiple_of(j * TQ, TQ), TQ)
            s = [scores(g, rows) for g in range(NSTREAM)]
            es, corrs = [], []
            for g in range(NSTREAM):
                cj = slope[g] * ((i - j) * TQ).astype(F32)
                m_old = m_ref[g]
                m_new = jnp.maximum(m_old, jnp.max(s[g], axis=0, keepdims=True) - cj)
                es.append(jnp.exp(s[g] - (m_new + cj)).astype(BF16))
                corrs.append(jnp.exp(m_old - m_new))
                m_ref[g] = m_new
            pvs = [jnp.dot(vt_ones(g, rows), es[g], preferred_element_type=F32)
                   for g in range(NSTREAM)]
            for g in range(NSTREAM):
                acc_ref[g] = corrs[g] * acc_ref[g] + pvs[g]
            return carry

        lax.fori_loop(0, i, body, 0)

    lv = lam_ref[...]
    t1 = jnp.sum(lv[0:1] * lv[1:2], axis=-1, keepdims=True)
    t2 = jnp.sum(lv[2:3] * lv[3:4], axis=-1, keepdims=True)
    lam = jnp.exp(t1) - jnp.exp(t2) + lambda_init

    for g in range(NSTREAM):
        acc = acc_ref[g]
        o_all = acc[:BLK] / acc[BLK:BLK + 1]
        o_t = o_all[:, :TQ] - lam * o_all[:, TQ:]
        ms = jnp.mean(o_t * o_t, axis=0, keepdims=True)
        o = (o_t * lax.rsqrt(ms + EPS)).T * g_ref[...] * (1.0 - lambda_init)
        o_ref[0, :, cols(g)] = o.astype(o_ref.dtype)


def _attn_c(proj, vt, slopes, lam_vecs, subln_g, lambda_init, seq):
    bsz, lp, _ = proj.shape
    gw = NSTREAM * BLK
    kcol = C_QK // gw
    return pl.pallas_call(
        functools.partial(_attn_c_kernel, seq=seq, lambda_init=lambda_init),
        grid=(bsz, C_QK // gw, pl.cdiv(lp, TQ)),
        in_specs=[
            pl.BlockSpec(memory_space=pltpu.SMEM),
            pl.BlockSpec((4, HEAD_DIM), lambda b, p, i: (0, 0)),
            pl.BlockSpec((1, 2 * HEAD_DIM), lambda b, p, i: (0, 0)),
            pl.BlockSpec((1, TQ, gw), lambda b, p, i: (b, i, p)),
            pl.BlockSpec((1, lp, gw), lambda b, p, i: (b, 0, kcol + p)),
            pl.BlockSpec((1, gw, lp), lambda b, p, i: (b, p, 0)),
        ],
        out_specs=pl.BlockSpec((1, TQ, gw), lambda b, p, i: (b, i, p)),
        out_shape=jax.ShapeDtypeStruct((bsz, lp, C_QK), BF16),
        scratch_shapes=[pltpu.VMEM((NSTREAM, 1, 2 * TQ), F32),
                        pltpu.VMEM((NSTREAM, ACC_ROWS, 2 * TQ), F32)],
        compiler_params=_params(("arbitrary", "arbitrary", "arbitrary")),
        name="attn_c",
    )(slopes, lam_vecs, subln_g.reshape(1, 2 * HEAD_DIM), proj, proj, vt)


def _alibi_slopes(n):
    return jnp.exp2(-8.0 * (jnp.arange(n, dtype=F32) + 1.0) / n)


def kernel(x, meta_tokens, ab_norm, w_in_ab, attn_sinks, w_out_ab, c_norm, w_in_c,
           diff_lambda, diff_subln, w_out_c, mlp_norm, w_mlp_in, w_mlp_out, final_norm):
    bsz, seq, d = x.shape
    lp = seq + BLK
    assert seq % TQ == 0 and d == D_MODEL
    t = bsz * lp
    depth = mlp_norm.shape[0]

    meta = jnp.broadcast_to(meta_tokens.astype(x.dtype)[None], (bsz, N_META, d))
    filler = jnp.zeros((bsz, BLK - N_META, d), x.dtype)
    h = jnp.concatenate([x, meta, filler], axis=1).reshape(t, d)

    tm = lp // 2

    for layer in range(depth):
        li = layer // 2
        if layer % 2 == 0:
            proj = _norm_matmul(h, ab_norm[li], w_in_ab, li, relu2=False, tm=tm, tn=512)
            proj = proj.reshape(bsz, lp, AB_IN)
            vt_a = jnp.swapaxes(proj[:, :, A_Q + A_KV:A_Q + 2 * A_KV], 1, 2)
            out_a = _attn_a(proj, vt_a, _alibi_slopes(A_HEADS), attn_sinks[li].astype(F32), seq)
            vt_b = jnp.swapaxes(proj[:, :, AB_IN - B_W:], 1, 2)
            out_b = _attn_b(proj, vt_b, seq)
            h = _matmul_res([out_a.reshape(t, A_Q), out_b.reshape(t, B_W)], w_out_ab, li, h,
                            tm=lp, tn=512, tk=A_Q + B_W)
        else:
            lambda_init = 0.8 - 0.6 * math.exp(-0.3 * layer)
            proj = _norm_matmul(h, c_norm[li], w_in_c, li, relu2=False, tm=tm, tn=1024)
            proj = proj.reshape(bsz, lp, 3 * C_QK)
            vt_c = jnp.swapaxes(proj[:, :, 2 * C_QK:], 1, 2)
            out_c = _attn_c(proj, vt_c, _alibi_slopes(C_HEADS), diff_lambda[li].astype(F32),
                            diff_subln[li].astype(F32), lambda_init, seq)
            h = _matmul_res([out_c.reshape(t, C_QK)], w_out_c, li, h, tm=lp, tn=512, tk=2048)
        u = _norm_matmul(h, mlp_norm[layer], w_mlp_in, layer, relu2=True, tm=lp, tn=512,
                         x_buffers=1)
        h = _matmul_res([u], w_mlp_out, layer, h, tm=lp, tn=512, tk=2048)

    return _final_norm(h.reshape(bsz, lp, d), final_norm, seq)
```

```python
import functools
import math

import jax
import jax.numpy as jnp
from jax import lax
from jax.experimental import pallas as pl
from jax.experimental.pallas import tpu as pltpu

D_MODEL = 2048
N_META = 16
CHUNK = 64
HEAD_DIM = 64
EPS = 1e-6
NEG = -1e30
HIDE = 1e32

A_HEADS = 16
A_KV_HEADS = 4
A_GROUP = A_HEADS // A_KV_HEADS
WINDOW_CHUNKS = 2
B_HEADS = 16
C_HEADS = D_MODEL // (2 * HEAD_DIM)
D_FF = 4 * D_MODEL

A_Q = A_HEADS * HEAD_DIM
A_KV = A_KV_HEADS * HEAD_DIM
B_W = B_HEADS * HEAD_DIM
AB_IN = A_Q + 2 * A_KV + 3 * B_W
C_QK = C_HEADS * 2 * HEAD_DIM

BLK = 128
TQ = 256
NSTREAM = 8
B_STREAMS = 8
B_PART = 4
ACC_ROWS = BLK + 16
SCALE = HEAD_DIM ** -0.5

VMEM_LIMIT = 56 * 1024 * 1024

F32 = jnp.float32
BF16 = jnp.bfloat16


def _params(sem, vmem=VMEM_LIMIT):
    return pltpu.CompilerParams(dimension_semantics=sem, vmem_limit_bytes=vmem)


def _nt_dot(a, b):
    return lax.dot_general(a, b, (((1,), (1,)), ((), ())), preferred_element_type=F32)


def _split_heads_rows(x):
    lane = lax.broadcasted_iota(jnp.int32, x.shape, 1)
    zero = jnp.zeros_like(x)
    return jnp.concatenate([jnp.where(lane < HEAD_DIM, x, zero),
                            jnp.where(lane >= HEAD_DIM, x, zero)], axis=0)


def _norm_matmul_kernel(x_ref, g_ref, w_ref, o_ref, hn_ref, *, relu2):
    @pl.when(pl.program_id(1) == 0)
    def _():
        x = x_ref[...]
        ms = jnp.mean(x * x, axis=-1, keepdims=True)
        hn_ref[...] = (x * lax.rsqrt(ms + EPS) * g_ref[...]).astype(hn_ref.dtype)

    acc = jnp.dot(hn_ref[...], w_ref[...].astype(BF16), preferred_element_type=F32)
    if relu2:
        acc = jnp.square(jnp.maximum(acc, 0.0))
    o_ref[...] = acc.astype(o_ref.dtype)


def _norm_matmul(x, g, w_stack, li, *, relu2, tm, tn):
    t, k = x.shape
    n = w_stack.shape[2]
    assert t % tm == 0 and n % tn == 0
    return pl.pallas_call(
        functools.partial(_norm_matmul_kernel, relu2=relu2),
        grid=(t // tm, n // tn),
        in_specs=[
            pl.BlockSpec((tm, k), lambda i, j: (i, 0)),
            pl.BlockSpec((1, k), lambda i, j: (0, 0)),
            pl.BlockSpec((None, k, tn), lambda i, j: (li, 0, j)),
        ],
        out_specs=pl.BlockSpec((tm, tn), lambda i, j: (i, j)),
        out_shape=jax.ShapeDtypeStruct((t, n), BF16),
        scratch_shapes=[pltpu.VMEM((tm, k), BF16)],
        compiler_params=_params(("arbitrary", "arbitrary")),
        name="norm_matmul_relu2" if relu2 else "norm_matmul",
    )(x, g.reshape(1, k), w_stack)


def _matmul_res_kernel(*refs, n_a):
    a_refs, (w_ref, r_ref, o_ref) = refs[:n_a], refs[n_a:]
    w = w_ref[...].astype(BF16)
    ka = w.shape[0] // n_a
    part = jnp.dot(a_refs[0][...], w[:ka], preferred_element_type=F32)
    for s in range(1, n_a):
        part += jnp.dot(a_refs[s][...], w[s * ka:(s + 1) * ka], preferred_element_type=F32)

    @pl.when(pl.program_id(2) == 0)
    def _():
        o_ref[...] = r_ref[...] + part

    @pl.when(pl.program_id(2) != 0)
    def _():
        o_ref[...] += part


def _matmul_res(a_parts, w_stack, li, res, *, tm, tn, tk):
    n_a = len(a_parts)
    t = a_parts[0].shape[0]
    _, k, n = w_stack.shape
    assert sum(a.shape[1] for a in a_parts) == k and (n_a == 1 or tk == k)
    assert t % tm == 0 and n % tn == 0 and k % tk == 0
    return pl.pallas_call(
        functools.partial(_matmul_res_kernel, n_a=n_a),
        grid=(t // tm, n // tn, k // tk),
        in_specs=[pl.BlockSpec((tm, tk // n_a), lambda i, j, kk: (i, kk)) for _ in a_parts] + [
            pl.BlockSpec((None, tk, tn), lambda i, j, kk: (li, kk, j)),
            pl.BlockSpec((tm, tn), lambda i, j, kk: (i, j)),
        ],
        out_specs=pl.BlockSpec((tm, tn), lambda i, j, kk: (i, j)),
        out_shape=jax.ShapeDtypeStruct((t, n), F32),
        compiler_params=_params(("arbitrary", "arbitrary", "arbitrary")),
        name="matmul_res",
    )(*a_parts, w_stack, res)


def _final_norm_kernel(x_ref, g_ref, o_ref):
    x = x_ref[0]
    ms = jnp.mean(x * x, axis=-1, keepdims=True)
    o_ref[0] = x * lax.rsqrt(ms + EPS) * g_ref[...]


def _final_norm(h3, g, seq):
    bsz, lp, d = h3.shape
    rows = 2 * BLK
    return pl.pallas_call(
        _final_norm_kernel,
        grid=(bsz, seq // rows),
        in_specs=[
            pl.BlockSpec((1, rows, d), lambda b, r: (b, r, 0)),
            pl.BlockSpec((1, d), lambda b, r: (0, 0)),
        ],
        out_specs=pl.BlockSpec((1, rows, d), lambda b, r: (b, r, 0)),
        out_shape=jax.ShapeDtypeStruct((bsz, seq, d), F32),
        compiler_params=_params(("arbitrary", "arbitrary")),
        name="final_norm",
    )(h3, g.reshape(1, d))


A_BAND = TQ + BLK
A_VROWS = HEAD_DIM + 16


def _attn_a_kernel(slopes_ref, sinks_ref, q_ref, k_ref, vt_ref, o_ref, *, seq):
    i = pl.program_id(1)
    lp = k_ref.shape[1]
    wl = A_GROUP * TQ
    start = pl.multiple_of(jnp.clip(i * TQ - BLK, 0, lp - A_BAND), BLK)
    band = pl.ds(start, A_BAND)
    meta = pl.ds(seq, N_META)

    def positions(nrows):
        r = lax.broadcasted_iota(jnp.int32, (nrows, TQ), 0)
        u_q = i * TQ + lax.broadcasted_iota(jnp.int32, (nrows, TQ), 1)
        q_frame = u_q < seq
        return r, jnp.where(q_frame, u_q + N_META, u_q - seq), jnp.where(q_frame, 1 + (u_q >> 6), 0)

    r, pos_q, qc = positions(A_BAND)
    u_k = start + r
    kc = 1 + (u_k >> 6)
    vis = (u_k < seq) & (kc <= qc) & (kc >= qc - WINDOW_CHUNKS)
    dist = jnp.where(vis, jnp.abs(pos_q - (u_k + N_META)).astype(F32), HIDE)
    rm, pos_qm, _ = positions(N_META)
    dist_m = jnp.abs(pos_qm - rm).astype(F32)
    head_lane = lax.broadcasted_iota(jnp.int32, (1, wl), 1) >> 8

    def per_head(ref, g):
        v = jnp.zeros((1, wl), F32)
        for a in range(A_GROUP):
            v = jnp.where(head_lane == a, ref[g * A_GROUP + a], v)
        return v

    def dup(kt, g):
        kg = kt[:, g * HEAD_DIM:(g + 1) * HEAD_DIM]
        return jnp.concatenate([kg, kg], axis=1)

    groups = range(A_KV_HEADS)
    k_band, k_meta = k_ref[0, band, :], k_ref[0, meta, :]
    qd = [jnp.concatenate(
        [_split_heads_rows(q_ref[0, :, (2 * g + p) * BLK:(2 * g + p + 1) * BLK] * SCALE)
         for p in range(2)], axis=0) for g in groups]
    s_b = [_nt_dot(dup(k_band, g), qd[g]) for g in groups]
    s_m = [_nt_dot(dup(k_meta, g), qd[g]) for g in groups]
    e_b, e_m, extra = [], [], []
    for g in groups:
        slope, sink = per_head(slopes_ref, g), per_head(sinks_ref, g)
        t_b = s_b[g] - slope * jnp.concatenate([dist] * A_GROUP, axis=1)
        t_m = s_m[g] - slope * jnp.concatenate([dist_m] * A_GROUP, axis=1)
        mx = jnp.maximum(jnp.maximum(jnp.max(t_b, axis=0, keepdims=True),
                                     jnp.max(t_m, axis=0, keepdims=True)), sink)
        e_b.append(jnp.exp(t_b - mx).astype(BF16))
        e_m.append(jnp.exp(t_m - mx).astype(BF16))
        extra.append(jnp.exp(sink - mx))

    def vt_ones(g, rows):
        return jnp.concatenate([vt_ref[0, g * HEAD_DIM:(g + 1) * HEAD_DIM, rows],
                                jnp.ones((A_VROWS - HEAD_DIM, rows.size), BF16)], axis=0)

    acc = [jnp.dot(vt_ones(g, band), e_b[g], preferred_element_type=F32)
           + jnp.dot(vt_ones(g, meta), e_m[g], preferred_element_type=F32) for g in groups]
    for g in groups:
        o_t = acc[g][:HEAD_DIM] / (acc[g][HEAD_DIM:HEAD_DIM + 1] + extra[g])
        for p in range(2):
            pair = jnp.concatenate([o_t[:, (2 * p) * TQ:(2 * p + 1) * TQ],
                                    o_t[:, (2 * p + 1) * TQ:(2 * p + 2) * TQ]], axis=0)
            o_ref[0, :, (2 * g + p) * BLK:(2 * g + p + 1) * BLK] = pair.T.astype(o_ref.dtype)


def _attn_a(proj, vt, slopes, sinks, seq):
    bsz, lp, _ = proj.shape
    kcol = A_Q // A_KV
    return pl.pallas_call(
        functools.partial(_attn_a_kernel, seq=seq),
        grid=(bsz, pl.cdiv(lp, TQ)),
        in_specs=[
            pl.BlockSpec(memory_space=pltpu.SMEM),
            pl.BlockSpec(memory_space=pltpu.SMEM),
            pl.BlockSpec((1, TQ, A_Q), lambda b, i: (b, i, 0)),
            pl.BlockSpec((1, lp, A_KV), lambda b, i: (b, 0, kcol)),
            pl.BlockSpec((1, A_KV, lp), lambda b, i: (b, 0, 0)),
        ],
        out_specs=pl.BlockSpec((1, TQ, A_Q), lambda b, i: (b, i, 0)),
        out_shape=jax.ShapeDtypeStruct((bsz, lp, A_Q), BF16),
        compiler_params=_params(("arbitrary", "arbitrary")),
        name="attn_a",
    )(slopes, sinks, proj, proj, vt)


def _attn_b_kernel(*refs, seq):
    n_part = B_STREAMS // B_PART
    q_refs, k_refs = refs[:n_part], refs[n_part:2 * n_part]
    vt_ref, o_ref, run_ref, acc_ref = refs[2 * n_part:]
    i = pl.program_id(2)
    n_ft = seq // TQ
    w = 2 * TQ

    def part(part_refs, g, rows):
        lo = (g % B_PART) * BLK
        return part_refs[g // B_PART][0, rows, lo:lo + BLK]

    streams = range(B_STREAMS)
    qd = [_split_heads_rows(part(q_refs, g, slice(None)) * SCALE)
          for g in streams]
    run_ref[...] = jnp.zeros_like(run_ref)
    acc_ref[...] = jnp.zeros_like(acc_ref)

    def tri(tk):
        s_idx = lax.broadcasted_iota(jnp.int32, (tk + 16, tk), 0)
        j_idx = lax.broadcasted_iota(jnp.int32, (tk + 16, tk), 1)
        return jnp.where((j_idx > s_idx) | (s_idx >= tk), 1.0, 0.0).astype(BF16)

    def tile(rows, tri_op, mask):
        zs = [_nt_dot(part(k_refs, g, rows), qd[g]) for g in streams]
        tk = rows.size
        sps, newers = [], []
        for g in streams:
            sp = jnp.maximum(zs[g], 0.0) + jnp.log(1.0 + jnp.exp(-jnp.abs(zs[g])))
            if mask is not None:
                sp = jnp.where(mask, sp, 0.0)
            newers.append(jnp.dot(tri_op, sp.astype(BF16), preferred_element_type=F32))
            sps.append(sp)
        pvs = []
        for g in streams:
            wgt = jnp.exp(zs[g] - (sps[g] + newers[g][:tk] + run_ref[g]))
            if mask is not None:
                wgt = jnp.where(mask, wgt, 0.0)
            pvs.append(jnp.dot(vt_ref[0, g * BLK:(g + 1) * BLK, rows], wgt.astype(BF16),
                               preferred_element_type=F32))
        for g in streams:
            acc_ref[g] += pvs[g]
            run_ref[g] += newers[g][tk:tk + 1]

    @pl.when(i < n_ft)
    def _():
        tri_tq = tri(TQ)
        r = lax.broadcasted_iota(jnp.int32, (TQ, w), 0)
        c = lax.broadcasted_iota(jnp.int32, (TQ, w), 1) & (TQ - 1)
        tile(pl.ds(pl.multiple_of(i * TQ, TQ), TQ), tri_tq, r < c)

        def body(jj, carry):
            tile(pl.ds(pl.multiple_of((i - 1 - jj) * TQ, TQ), TQ), tri_tq, None)
            return carry

        lax.fori_loop(0, i, body, 0)

    r = lax.broadcasted_iota(jnp.int32, (N_META, w), 0)
    c = lax.broadcasted_iota(jnp.int32, (N_META, w), 1) & (TQ - 1)
    tile(pl.ds(seq, N_META), tri(N_META), (i < n_ft) | (r < c))

    for g in streams:
        acc = acc_ref[g]
        o_t = jnp.concatenate([acc[:HEAD_DIM, :TQ], acc[HEAD_DIM:, TQ:]], axis=0)
        o_ref[0, :, g * BLK:(g + 1) * BLK] = o_t.T.astype(o_ref.dtype)


def _attn_b(proj, vt, seq):
    bsz, lp, _ = proj.shape
    gw = B_STREAMS * BLK
    pw = B_PART * BLK
    n_part = B_STREAMS // B_PART
    assert (A_Q + 2 * A_KV) % pw == 0 and B_W % gw == 0 and B_STREAMS % B_PART == 0
    qcol = (A_Q + 2 * A_KV) // pw
    kcol = qcol + B_W // pw

    def col_specs(rows, first, row_index):
        return [pl.BlockSpec((1, rows, pw),
                             lambda b, p, i, s=s: (b, row_index(i), first + p * n_part + s))
                for s in range(n_part)]

    return pl.pallas_call(
        functools.partial(_attn_b_kernel, seq=seq),
        grid=(bsz, B_W // gw, pl.cdiv(lp, TQ)),
        in_specs=col_specs(TQ, qcol, lambda i: i) + col_specs(lp, kcol, lambda i: 0) + [
            pl.BlockSpec((1, gw, lp), lambda b, p, i: (b, p, 0)),
        ],
        out_specs=pl.BlockSpec((1, TQ, gw), lambda b, p, i: (b, i, p)),
        out_shape=jax.ShapeDtypeStruct((bsz, lp, B_W), BF16),
        scratch_shapes=[pltpu.VMEM((B_STREAMS, 1, 2 * TQ), F32),
                        pltpu.VMEM((B_STREAMS, BLK, 2 * TQ), F32)],
        compiler_params=_params(("arbitrary", "arbitrary", "arbitrary")),
        name="attn_b",
    )(*([proj] * (2 * n_part)), vt)


def _attn_c_kernel(slopes_ref, lam_ref, g_ref, q_ref, k_ref, vt_ref, o_ref,
                   m_ref, acc_ref, *, seq, lambda_init):
    p = pl.program_id(1)
    i = pl.program_id(2)
    n_ft = seq // TQ
    w = 2 * TQ
    slope = [slopes_ref[p * NSTREAM + g] for g in range(NSTREAM)]
    qd = [_split_heads_rows(q_ref[0, :, g * BLK:(g + 1) * BLK] * SCALE)
          for g in range(NSTREAM)]
    meta = pl.ds(seq, N_META)

    def cols(g):
        return slice(g * BLK, (g + 1) * BLK)

    def vt_ones(g, rows):
        tk = rows.size
        return jnp.concatenate([vt_ref[0, cols(g), rows], jnp.ones((ACC_ROWS - BLK, tk), BF16)],
                               axis=0)

    def meta_scores(g):
        return _nt_dot(k_ref[0, meta, cols(g)], qd[g])

    rm = lax.broadcasted_iota(jnp.int32, (N_META, w), 0)
    cm = lax.broadcasted_iota(jnp.int32, (N_META, w), 1) & (TQ - 1)

    @pl.when(i == n_ft)
    def _():
        dist = jnp.abs(cm - rm).astype(F32)
        for g in range(NSTREAM):
            t = meta_scores(g) - slope[g] * dist
            m0 = jnp.max(t, axis=0, keepdims=True)
            e0 = jnp.exp(t - m0)
            m_ref[g] = m0
            acc_ref[g] = jnp.dot(vt_ones(g, meta), e0.astype(BF16), preferred_element_type=F32)

    @pl.when(i < n_ft)
    def _():
        r = lax.broadcasted_iota(jnp.int32, (TQ, w), 0)
        c = lax.broadcasted_iota(jnp.int32, (TQ, w), 1) & (TQ - 1)
        lane_k = lax.broadcasted_iota(jnp.int32, (TQ, BLK), 1)
        row_k = lax.broadcasted_iota(jnp.int32, (TQ, BLK), 0)
        key_off = jnp.where(lane_k < 3, row_k, 0).astype(F32).astype(BF16)
        lane_q = lax.broadcasted_iota(jnp.int32, (16, BLK), 1)
        qa = []
        for g in range(NSTREAM):
            sl = jnp.full((16, BLK), slope[g], F32)
            s_hi = sl.astype(BF16).astype(F32)
            s_mid = (sl - s_hi).astype(BF16).astype(F32)
            s_lo = (sl - s_hi) - s_mid
            sl3 = jnp.where(lane_q == 0, s_hi,
                            jnp.where(lane_q == 1, s_mid, jnp.where(lane_q == 2, s_lo, 0.0)))
            sl3 = jnp.concatenate([sl3.astype(BF16)] * (w // 16), axis=0)
            qa.append(jnp.concatenate([qd[g], sl3], axis=1))

        def scores(g, rows):
            return _nt_dot(jnp.concatenate([k_ref[0, rows, cols(g)], key_off], axis=1), qa[g])

        diag = pl.ds(pl.multiple_of(i * TQ, TQ), TQ)
        vis = (r >> 6) <= (c >> 6)
        ahead = jnp.where(vis, 2.0 * jnp.maximum(r - c, 0).astype(F32), HIDE)
        meta_off = (rm - (N_META + i * TQ)).astype(F32)
        s_d = [scores(g, diag) for g in range(NSTREAM)]
        s_m = [meta_scores(g) for g in range(NSTREAM)]
        e_d, e_m = [], []
        for g in range(NSTREAM):
            t_d = s_d[g] - slope[g] * ahead
            t_m = s_m[g] + slope[g] * meta_off
            m0 = jnp.maximum(jnp.max(t_d, axis=0, keepdims=True),
                             jnp.max(t_m, axis=0, keepdims=True))
            e_d.append(jnp.exp(t_d - m0).astype(BF16))
            e_m.append(jnp.exp(t_m - m0).astype(BF16))
            m_ref[g] = m0
        for g in range(NSTREAM):
            acc_ref[g] = (jnp.dot(vt_ones(g, diag), e_d[g], preferred_element_type=F32)
                          + jnp.dot(vt_ones(g, meta), e_m[g], preferred_element_type=F32))

        def body(j, carry):
            rows = pl.ds(pl.multiple_of(j * TQ, TQ), TQ)
            s = [scores(g, rows) for g in range(NSTREAM)]
            es, corrs = [], []
            for g in range(NSTREAM):
                cj = slope[g] * ((i - j) * TQ).astype(F32)
                m_old = m_ref[g]
                m_new = jnp.maximum(m_old, jnp.max(s[g], axis=0, keepdims=True) - cj)
                es.append(jnp.exp(s[g] - (m_new + cj)).astype(BF16))
                corrs.append(jnp.exp(m_old - m_new))
                m_ref[g] = m_new
            pvs = [jnp.dot(vt_ones(g, rows), es[g], preferred_element_type=F32)
                   for g in range(NSTREAM)]
            for g in range(NSTREAM):
                acc_ref[g] = corrs[g] * acc_ref[g] + pvs[g]
            return carry

        lax.fori_loop(0, i, body, 0)

    lv = lam_ref[...]
    t1 = jnp.sum(lv[0:1] * lv[1:2], axis=-1, keepdims=True)
    t2 = jnp.sum(lv[2:3] * lv[3:4], axis=-1, keepdims=True)
    lam = jnp.exp(t1) - jnp.exp(t2) + lambda_init

    for g in range(NSTREAM):
        acc = acc_ref[g]
        o_all = acc[:BLK] / acc[BLK:BLK + 1]
        o_t = o_all[:, :TQ] - lam * o_all[:, TQ:]
        ms = jnp.mean(o_t * o_t, axis=0, keepdims=True)
        o = (o_t * lax.rsqrt(ms + EPS)).T * g_ref[...] * (1.0 - lambda_init)
        o_ref[0, :, cols(g)] = o.astype(o_ref.dtype)


def _attn_c(proj, vt, slopes, lam_vecs, subln_g, lambda_init, seq):
    bsz, lp, _ = proj.shape
    gw = NSTREAM * BLK
    kcol = C_QK // gw
    return pl.pallas_call(
        functools.partial(_attn_c_kernel, seq=seq, lambda_init=lambda_init),
        grid=(bsz, C_QK // gw, pl.cdiv(lp, TQ)),
        in_specs=[
            pl.BlockSpec(memory_space=pltpu.SMEM),
            pl.BlockSpec((4, HEAD_DIM), lambda b, p, i: (0, 0)),
            pl.BlockSpec((1, 2 * HEAD_DIM), lambda b, p, i: (0, 0)),
            pl.BlockSpec((1, TQ, gw), lambda b, p, i: (b, i, p)),
            pl.BlockSpec((1, lp, gw), lambda b, p, i: (b, 0, kcol + p)),
            pl.BlockSpec((1, gw, lp), lambda b, p, i: (b, p, 0)),
        ],
        out_specs=pl.BlockSpec((1, TQ, gw), lambda b, p, i: (b, i, p)),
        out_shape=jax.ShapeDtypeStruct((bsz, lp, C_QK), BF16),
        scratch_shapes=[pltpu.VMEM((NSTREAM, 1, 2 * TQ), F32),
                        pltpu.VMEM((NSTREAM, ACC_ROWS, 2 * TQ), F32)],
        compiler_params=_params(("arbitrary", "arbitrary", "arbitrary")),
        name="attn_c",
    )(slopes, lam_vecs, subln_g.reshape(1, 2 * HEAD_DIM), proj, proj, vt)


def _alibi_slopes(n):
    return jnp.exp2(-8.0 * (jnp.arange(n, dtype=F32) + 1.0) / n)


def kernel(x, meta_tokens, ab_norm, w_in_ab, attn_sinks, w_out_ab, c_norm, w_in_c,
           diff_lambda, diff_subln, w_out_c, mlp_norm, w_mlp_in, w_mlp_out, final_norm):
    bsz, seq, d = x.shape
    lp = seq + BLK
    assert seq % TQ == 0 and d == D_MODEL
    t = bsz * lp
    depth = mlp_norm.shape[0]

    meta = jnp.broadcast_to(meta_tokens.astype(x.dtype)[None], (bsz, N_META, d))
    filler = jnp.zeros((bsz, BLK - N_META, d), x.dtype)
    h = jnp.concatenate([x, meta, filler], axis=1).reshape(t, d)

    tm = lp // 2

    for layer in range(depth):
        li = layer // 2
        if layer % 2 == 0:
            proj = _norm_matmul(h, ab_norm[li], w_in_ab, li, relu2=False, tm=tm, tn=512)
            proj = proj.reshape(bsz, lp, AB_IN)
            vt_a = jnp.swapaxes(proj[:, :, A_Q + A_KV:A_Q + 2 * A_KV], 1, 2)
            out_a = _attn_a(proj, vt_a, _alibi_slopes(A_HEADS), attn_sinks[li].astype(F32), seq)
            vt_b = jnp.swapaxes(proj[:, :, AB_IN - B_W:], 1, 2)
            out_b = _attn_b(proj, vt_b, seq)
            h = _matmul_res([out_a.reshape(t, A_Q), out_b.reshape(t, B_W)], w_out_ab, li, h,
                            tm=lp, tn=512, tk=A_Q + B_W)
        else:
            lambda_init = 0.8 - 0.6 * math.exp(-0.3 * layer)
            proj = _norm_matmul(h, c_norm[li], w_in_c, li, relu2=False, tm=tm, tn=1024)
            proj = proj.reshape(bsz, lp, 3 * C_QK)
            vt_c = jnp.swapaxes(proj[:, :, 2 * C_QK:], 1, 2)
            out_c = _attn_c(proj, vt_c, _alibi_slopes(C_HEADS), diff_lambda[li].astype(F32),
                            diff_subln[li].astype(F32), lambda_init, seq)
            h = _matmul_res([out_c.reshape(t, C_QK)], w_out_c, li, h, tm=lp, tn=512, tk=2048)
        u = _norm_matmul(h, mlp_norm[layer], w_mlp_in, layer, relu2=True, tm=tm, tn=1024)
        h = _matmul_res([u], w_mlp_out, layer, h, tm=lp, tn=512, tk=2048)

    return _final_norm(h.reshape(bsz, lp, d), final_norm, seq)
```

```python
import functools
import math

import jax
import jax.numpy as jnp
from jax import lax
from jax.experimental import pallas as pl
from jax.experimental.pallas import tpu as pltpu

D_MODEL = 2048
N_META = 16
CHUNK = 64
HEAD_DIM = 64
EPS = 1e-6
NEG = -1e30
HIDE = 1e32

A_HEADS = 16
A_KV_HEADS = 4
A_GROUP = A_HEADS // A_KV_HEADS
WINDOW_CHUNKS = 2
B_HEADS = 16
C_HEADS = D_MODEL // (2 * HEAD_DIM)
D_FF = 4 * D_MODEL

A_Q = A_HEADS * HEAD_DIM
A_KV = A_KV_HEADS * HEAD_DIM
B_W = B_HEADS * HEAD_DIM
AB_IN = A_Q + 2 * A_KV + 3 * B_W
C_QK = C_HEADS * 2 * HEAD_DIM

BLK = 128
TQ = 256
NSTREAM = 8
B_STREAMS = 8
B_PART = 4
ACC_ROWS = BLK + 16
SCALE = HEAD_DIM ** -0.5

VMEM_LIMIT = 56 * 1024 * 1024

F32 = jnp.float32
BF16 = jnp.bfloat16


def _params(sem, vmem=VMEM_LIMIT):
    return pltpu.CompilerParams(dimension_semantics=sem, vmem_limit_bytes=vmem)


def _nt_dot(a, b):
    return lax.dot_general(a, b, (((1,), (1,)), ((), ())), preferred_element_type=F32)


def _split_heads_rows(x):
    lane = lax.broadcasted_iota(jnp.int32, x.shape, 1)
    zero = jnp.zeros_like(x)
    return jnp.concatenate([jnp.where(lane < HEAD_DIM, x, zero),
                            jnp.where(lane >= HEAD_DIM, x, zero)], axis=0)


def _norm_matmul_kernel(x_ref, g_ref, w_ref, o_ref, hn_ref, *, relu2):
    @pl.when(pl.program_id(1) == 0)
    def _():
        x = x_ref[...]
        ms = jnp.mean(x * x, axis=-1, keepdims=True)
        hn_ref[...] = (x * lax.rsqrt(ms + EPS) * g_ref[...]).astype(hn_ref.dtype)

    acc = jnp.dot(hn_ref[...], w_ref[...].astype(BF16), preferred_element_type=F32)
    if relu2:
        acc = jnp.square(jnp.maximum(acc, 0.0))
    o_ref[...] = acc.astype(o_ref.dtype)


def _norm_matmul(x, g, w_stack, li, *, relu2, tm, tn):
    t, k = x.shape
    n = w_stack.shape[2]
    assert t % tm == 0 and n % tn == 0
    return pl.pallas_call(
        functools.partial(_norm_matmul_kernel, relu2=relu2),
        grid=(t // tm, n // tn),
        in_specs=[
            pl.BlockSpec((tm, k), lambda i, j: (i, 0)),
            pl.BlockSpec((1, k), lambda i, j: (0, 0)),
            pl.BlockSpec((None, k, tn), lambda i, j: (li, 0, j)),
        ],
        out_specs=pl.BlockSpec((tm, tn), lambda i, j: (i, j)),
        out_shape=jax.ShapeDtypeStruct((t, n), BF16),
        scratch_shapes=[pltpu.VMEM((tm, k), BF16)],
        compiler_params=_params(("arbitrary", "arbitrary")),
        name="norm_matmul_relu2" if relu2 else "norm_matmul",
    )(x, g.reshape(1, k), w_stack)


def _matmul_res_kernel(*refs, n_a):
    a_refs, (w_ref, r_ref, o_ref) = refs[:n_a], refs[n_a:]
    w = w_ref[...].astype(BF16)
    ka = w.shape[0] // n_a
    part = jnp.dot(a_refs[0][...], w[:ka], preferred_element_type=F32)
    for s in range(1, n_a):
        part += jnp.dot(a_refs[s][...], w[s * ka:(s + 1) * ka], preferred_element_type=F32)

    @pl.when(pl.program_id(2) == 0)
    def _():
        o_ref[...] = r_ref[...] + part

    @pl.when(pl.program_id(2) != 0)
    def _():
        o_ref[...] += part


def _matmul_res(a_parts, w_stack, li, res, *, tm, tn, tk):
    n_a = len(a_parts)
    t = a_parts[0].shape[0]
    _, k, n = w_stack.shape
    assert sum(a.shape[1] for a in a_parts) == k and (n_a == 1 or tk == k)
    assert t % tm == 0 and n % tn == 0 and k % tk == 0
    return pl.pallas_call(
        functools.partial(_matmul_res_kernel, n_a=n_a),
        grid=(t // tm, n // tn, k // tk),
        in_specs=[pl.BlockSpec((tm, tk // n_a), lambda i, j, kk: (i, kk)) for _ in a_parts] + [
            pl.BlockSpec((None, tk, tn), lambda i, j, kk: (li, kk, j)),
            pl.BlockSpec((tm, tn), lambda i, j, kk: (i, j)),
        ],
        out_specs=pl.BlockSpec((tm, tn), lambda i, j, kk: (i, j)),
        out_shape=jax.ShapeDtypeStruct((t, n), F32),
        compiler_params=_params(("arbitrary", "arbitrary", "arbitrary")),
        name="matmul_res",
    )(*a_parts, w_stack, res)


def _final_norm_kernel(x_ref, g_ref, o_ref):
    x = x_ref[0]
    ms = jnp.mean(x * x, axis=-1, keepdims=True)
    o_ref[0] = x * lax.rsqrt(ms + EPS) * g_ref[...]


def _final_norm(h3, g, seq):
    bsz, lp, d = h3.shape
    rows = 2 * BLK
    return pl.pallas_call(
        _final_norm_kernel,
        grid=(bsz, seq // rows),
        in_specs=[
            pl.BlockSpec((1, rows, d), lambda b, r: (b, r, 0)),
            pl.BlockSpec((1, d), lambda b, r: (0, 0)),
        ],
        out_specs=pl.BlockSpec((1, rows, d), lambda b, r: (b, r, 0)),
        out_shape=jax.ShapeDtypeStruct((bsz, seq, d), F32),
        compiler_params=_params(("arbitrary", "arbitrary")),
        name="final_norm",
    )(h3, g.reshape(1, d))


def _transpose_kernel(x_ref, o_ref):
    for c in range(x_ref.shape[2] // BLK):
        o_ref[0, c * BLK:(c + 1) * BLK, :] = x_ref[0, :, c * BLK:(c + 1) * BLK].T


def _transposed_cols(proj, col0, ncols, cw=2 * BLK):
    bsz, lp, _ = proj.shape
    assert col0 % cw == 0 and ncols % cw == 0
    return pl.pallas_call(
        _transpose_kernel,
        grid=(bsz, ncols // cw),
        in_specs=[pl.BlockSpec((1, lp, cw), lambda b, c: (b, 0, col0 // cw + c))],
        out_specs=pl.BlockSpec((1, cw, lp), lambda b, c: (b, c, 0)),
        out_shape=jax.ShapeDtypeStruct((bsz, ncols, lp), proj.dtype),
        compiler_params=_params(("arbitrary", "arbitrary")),
        name="transpose_v",
    )(proj)


A_BAND = TQ + BLK
A_VROWS = HEAD_DIM + 16


def _attn_a_kernel(slopes_ref, sinks_ref, q_ref, k_ref, vt_ref, o_ref, *, seq):
    i = pl.program_id(1)
    lp = k_ref.shape[1]
    wl = A_GROUP * TQ
    start = pl.multiple_of(jnp.clip(i * TQ - BLK, 0, lp - A_BAND), BLK)
    band = pl.ds(start, A_BAND)
    meta = pl.ds(seq, N_META)

    def positions(nrows):
        r = lax.broadcasted_iota(jnp.int32, (nrows, TQ), 0)
        u_q = i * TQ + lax.broadcasted_iota(jnp.int32, (nrows, TQ), 1)
        q_frame = u_q < seq
        return r, jnp.where(q_frame, u_q + N_META, u_q - seq), jnp.where(q_frame, 1 + (u_q >> 6), 0)

    r, pos_q, qc = positions(A_BAND)
    u_k = start + r
    kc = 1 + (u_k >> 6)
    vis = (u_k < seq) & (kc <= qc) & (kc >= qc - WINDOW_CHUNKS)
    dist = jnp.where(vis, jnp.abs(pos_q - (u_k + N_META)).astype(F32), HIDE)
    rm, pos_qm, _ = positions(N_META)
    dist_m = jnp.abs(pos_qm - rm).astype(F32)
    head_lane = lax.broadcasted_iota(jnp.int32, (1, wl), 1) >> 8

    def per_head(ref, g):
        v = jnp.zeros((1, wl), F32)
        for a in range(A_GROUP):
            v = jnp.where(head_lane == a, ref[g * A_GROUP + a], v)
        return v

    def dup(kt, g):
        kg = kt[:, g * HEAD_DIM:(g + 1) * HEAD_DIM]
        return jnp.concatenate([kg, kg], axis=1)

    groups = range(A_KV_HEADS)
    k_band, k_meta = k_ref[0, band, :], k_ref[0, meta, :]
    qd = [jnp.concatenate(
        [_split_heads_rows(q_ref[0, :, (2 * g + p) * BLK:(2 * g + p + 1) * BLK] * SCALE)
         for p in range(2)], axis=0) for g in groups]
    s_b = [_nt_dot(dup(k_band, g), qd[g]) for g in groups]
    s_m = [_nt_dot(dup(k_meta, g), qd[g]) for g in groups]
    e_b, e_m, extra = [], [], []
    for g in groups:
        slope, sink = per_head(slopes_ref, g), per_head(sinks_ref, g)
        t_b = s_b[g] - slope * jnp.concatenate([dist] * A_GROUP, axis=1)
        t_m = s_m[g] - slope * jnp.concatenate([dist_m] * A_GROUP, axis=1)
        mx = jnp.maximum(jnp.maximum(jnp.max(t_b, axis=0, keepdims=True),
                                     jnp.max(t_m, axis=0, keepdims=True)), sink)
        e_b.append(jnp.exp(t_b - mx).astype(BF16))
        e_m.append(jnp.exp(t_m - mx).astype(BF16))
        extra.append(jnp.exp(sink - mx))

    def vt_ones(g, rows):
        return jnp.concatenate([vt_ref[0, g * HEAD_DIM:(g + 1) * HEAD_DIM, rows],
                                jnp.ones((A_VROWS - HEAD_DIM, rows.size), BF16)], axis=0)

    acc = [jnp.dot(vt_ones(g, band), e_b[g], preferred_element_type=F32)
           + jnp.dot(vt_ones(g, meta), e_m[g], preferred_element_type=F32) for g in groups]
    for g in groups:
        o_t = acc[g][:HEAD_DIM] / (acc[g][HEAD_DIM:HEAD_DIM + 1] + extra[g])
        for p in range(2):
            pair = jnp.concatenate([o_t[:, (2 * p) * TQ:(2 * p + 1) * TQ],
                                    o_t[:, (2 * p + 1) * TQ:(2 * p + 2) * TQ]], axis=0)
            o_ref[0, :, (2 * g + p) * BLK:(2 * g + p + 1) * BLK] = pair.T.astype(o_ref.dtype)


def _attn_a(proj, vt, slopes, sinks, seq):
    bsz, lp, _ = proj.shape
    kcol = A_Q // A_KV
    return pl.pallas_call(
        functools.partial(_attn_a_kernel, seq=seq),
        grid=(bsz, pl.cdiv(lp, TQ)),
        in_specs=[
            pl.BlockSpec(memory_space=pltpu.SMEM),
            pl.BlockSpec(memory_space=pltpu.SMEM),
            pl.BlockSpec((1, TQ, A_Q), lambda b, i: (b, i, 0)),
            pl.BlockSpec((1, lp, A_KV), lambda b, i: (b, 0, kcol)),
            pl.BlockSpec((1, A_KV, lp), lambda b, i: (b, 0, 0)),
        ],
        out_specs=pl.BlockSpec((1, TQ, A_Q), lambda b, i: (b, i, 0)),
        out_shape=jax.ShapeDtypeStruct((bsz, lp, A_Q), BF16),
        compiler_params=_params(("arbitrary", "arbitrary")),
        name="attn_a",
    )(slopes, sinks, proj, proj, vt)


def _attn_b_kernel(*refs, seq):
    n_part = B_STREAMS // B_PART
    q_refs, k_refs = refs[:n_part], refs[n_part:2 * n_part]
    vt_ref, o_ref, run_ref, acc_ref = refs[2 * n_part:]
    i = pl.program_id(2)
    n_ft = seq // TQ
    w = 2 * TQ

    def part(part_refs, g, rows):
        lo = (g % B_PART) * BLK
        return part_refs[g // B_PART][0, rows, lo:lo + BLK]

    streams = range(B_STREAMS)
    qd = [_split_heads_rows(part(q_refs, g, slice(None)) * SCALE)
          for g in streams]
    run_ref[...] = jnp.zeros_like(run_ref)
    acc_ref[...] = jnp.zeros_like(acc_ref)

    def tri(tk):
        s_idx = lax.broadcasted_iota(jnp.int32, (tk + 16, tk), 0)
        j_idx = lax.broadcasted_iota(jnp.int32, (tk + 16, tk), 1)
        return jnp.where((j_idx > s_idx) | (s_idx >= tk), 1.0, 0.0).astype(BF16)

    def tile(rows, tri_op, mask):
        zs = [_nt_dot(part(k_refs, g, rows), qd[g]) for g in streams]
        tk = rows.size
        sps, newers = [], []
        for g in streams:
            sp = jnp.maximum(zs[g], 0.0) + jnp.log(1.0 + jnp.exp(-jnp.abs(zs[g])))
            if mask is not None:
                sp = jnp.where(mask, sp, 0.0)
            newers.append(jnp.dot(tri_op, sp.astype(BF16), preferred_element_type=F32))
            sps.append(sp)
        pvs = []
        for g in streams:
            wgt = jnp.exp(zs[g] - (sps[g] + newers[g][:tk] + run_ref[g]))
            if mask is not None:
                wgt = jnp.where(mask, wgt, 0.0)
            pvs.append(jnp.dot(vt_ref[0, g * BLK:(g + 1) * BLK, rows], wgt.astype(BF16),
                               preferred_element_type=F32))
        for g in streams:
            acc_ref[g] += pvs[g]
            run_ref[g] += newers[g][tk:tk + 1]

    @pl.when(i < n_ft)
    def _():
        tri_tq = tri(TQ)
        r = lax.broadcasted_iota(jnp.int32, (TQ, w), 0)
        c = lax.broadcasted_iota(jnp.int32, (TQ, w), 1) & (TQ - 1)
        tile(pl.ds(pl.multiple_of(i * TQ, TQ), TQ), tri_tq, r < c)

        def body(jj, carry):
            tile(pl.ds(pl.multiple_of((i - 1 - jj) * TQ, TQ), TQ), tri_tq, None)
            return carry

        lax.fori_loop(0, i, body, 0)

    r = lax.broadcasted_iota(jnp.int32, (N_META, w), 0)
    c = lax.broadcasted_iota(jnp.int32, (N_META, w), 1) & (TQ - 1)
    tile(pl.ds(seq, N_META), tri(N_META), (i < n_ft) | (r < c))

    for g in streams:
        acc = acc_ref[g]
        o_t = jnp.concatenate([acc[:HEAD_DIM, :TQ], acc[HEAD_DIM:, TQ:]], axis=0)
        o_ref[0, :, g * BLK:(g + 1) * BLK] = o_t.T.astype(o_ref.dtype)


def _attn_b(proj, vt, seq):
    bsz, lp, _ = proj.shape
    gw = B_STREAMS * BLK
    pw = B_PART * BLK
    n_part = B_STREAMS // B_PART
    assert (A_Q + 2 * A_KV) % pw == 0 and B_W % gw == 0 and B_STREAMS % B_PART == 0
    qcol = (A_Q + 2 * A_KV) // pw
    kcol = qcol + B_W // pw

    def col_specs(rows, first, row_index):
        return [pl.BlockSpec((1, rows, pw),
                             lambda b, p, i, s=s: (b, row_index(i), first + p * n_part + s))
                for s in range(n_part)]

    return pl.pallas_call(
        functools.partial(_attn_b_kernel, seq=seq),
        grid=(bsz, B_W // gw, pl.cdiv(lp, TQ)),
        in_specs=col_specs(TQ, qcol, lambda i: i) + col_specs(lp, kcol, lambda i: 0) + [
            pl.BlockSpec((1, gw, lp), lambda b, p, i: (b, p, 0)),
        ],
        out_specs=pl.BlockSpec((1, TQ, gw), lambda b, p, i: (b, i, p)),
        out_shape=jax.ShapeDtypeStruct((bsz, lp, B_W), BF16),
        scratch_shapes=[pltpu.VMEM((B_STREAMS, 1, 2 * TQ), F32),
                        pltpu.VMEM((B_STREAMS, BLK, 2 * TQ), F32)],
        compiler_params=_params(("arbitrary", "arbitrary", "arbitrary")),
        name="attn_b",
    )(*([proj] * (2 * n_part)), vt)


def _attn_c_kernel(slopes_ref, lam_ref, g_ref, q_ref, k_ref, vt_ref, o_ref,
                   m_ref, acc_ref, *, seq, lambda_init):
    p = pl.program_id(1)
    i = pl.program_id(2)
    n_ft = seq // TQ
    w = 2 * TQ
    slope = [slopes_ref[p * NSTREAM + g] for g in range(NSTREAM)]
    qd = [_split_heads_rows(q_ref[0, :, g * BLK:(g + 1) * BLK] * SCALE)
          for g in range(NSTREAM)]
    meta = pl.ds(seq, N_META)

    def cols(g):
        return slice(g * BLK, (g + 1) * BLK)

    def vt_ones(g, rows):
        tk = rows.size
        return jnp.concatenate([vt_ref[0, cols(g), rows], jnp.ones((ACC_ROWS - BLK, tk), BF16)],
                               axis=0)

    def meta_scores(g):
        return _nt_dot(k_ref[0, meta, cols(g)], qd[g])

    rm = lax.broadcasted_iota(jnp.int32, (N_META, w), 0)
    cm = lax.broadcasted_iota(jnp.int32, (N_META, w), 1) & (TQ - 1)

    @pl.when(i == n_ft)
    def _():
        dist = jnp.abs(cm - rm).astype(F32)
        for g in range(NSTREAM):
            t = meta_scores(g) - slope[g] * dist
            m0 = jnp.max(t, axis=0, keepdims=True)
            e0 = jnp.exp(t - m0)
            m_ref[g] = m0
            acc_ref[g] = jnp.dot(vt_ones(g, meta), e0.astype(BF16), preferred_element_type=F32)

    @pl.when(i < n_ft)
    def _():
        r = lax.broadcasted_iota(jnp.int32, (TQ, w), 0)
        c = lax.broadcasted_iota(jnp.int32, (TQ, w), 1) & (TQ - 1)
        lane_k = lax.broadcasted_iota(jnp.int32, (TQ, BLK), 1)
        row_k = lax.broadcasted_iota(jnp.int32, (TQ, BLK), 0)
        key_off = jnp.where(lane_k < 3, row_k, 0).astype(F32).astype(BF16)
        lane_q = lax.broadcasted_iota(jnp.int32, (16, BLK), 1)
        qa = []
        for g in range(NSTREAM):
            sl = jnp.full((16, BLK), slope[g], F32)
            s_hi = sl.astype(BF16).astype(F32)
            s_mid = (sl - s_hi).astype(BF16).astype(F32)
            s_lo = (sl - s_hi) - s_mid
            sl3 = jnp.where(lane_q == 0, s_hi,
                            jnp.where(lane_q == 1, s_mid, jnp.where(lane_q == 2, s_lo, 0.0)))
            sl3 = jnp.concatenate([sl3.astype(BF16)] * (w // 16), axis=0)
            qa.append(jnp.concatenate([qd[g], sl3], axis=1))

        def scores(g, rows):
            return _nt_dot(jnp.concatenate([k_ref[0, rows, cols(g)], key_off], axis=1), qa[g])

        diag = pl.ds(pl.multiple_of(i * TQ, TQ), TQ)
        vis = (r >> 6) <= (c >> 6)
        ahead = jnp.where(vis, 2.0 * jnp.maximum(r - c, 0).astype(F32), HIDE)
        meta_off = (rm - (N_META + i * TQ)).astype(F32)
        s_d = [scores(g, diag) for g in range(NSTREAM)]
        s_m = [meta_scores(g) for g in range(NSTREAM)]
        e_d, e_m = [], []
        for g in range(NSTREAM):
            t_d = s_d[g] - slope[g] * ahead
            t_m = s_m[g] + slope[g] * meta_off
            m0 = jnp.maximum(jnp.max(t_d, axis=0, keepdims=True),
                             jnp.max(t_m, axis=0, keepdims=True))
            e_d.append(jnp.exp(t_d - m0).astype(BF16))
            e_m.append(jnp.exp(t_m - m0).astype(BF16))
            m_ref[g] = m0
        for g in range(NSTREAM):
            acc_ref[g] = (jnp.dot(vt_ones(g, diag), e_d[g], preferred_element_type=F32)
                          + jnp.dot(vt_ones(g, meta), e_m[g], preferred_element_type=F32))

        def body(j, carry):
            rows = pl.ds(pl.multiple_of(j * TQ, TQ), TQ)
            s = [scores(g, rows) for g in range(NSTREAM)]
            es, corrs = [], []
            for g in range(NSTREAM):
                cj = slope[g] * ((i - j) * TQ).astype(F32)
                m_old = m_ref[g]
                m_new = jnp.maximum(m_old, jnp.max(s[g], axis=0, keepdims=True) - cj)
                es.append(jnp.exp(s[g] - (m_new + cj)).astype(BF16))
                corrs.append(jnp.exp(m_old - m_new))
                m_ref[g] = m_new
            pvs = [jnp.dot(vt_ones(g, rows), es[g], preferred_element_type=F32)
                   for g in range(NSTREAM)]
            for g in range(NSTREAM):
                acc_ref[g] = corrs[g] * acc_ref[g] + pvs[g]
            return carry

        lax.fori_loop(0, i, body, 0)

    lv = lam_ref[...]
    t1 = jnp.sum(lv[0:1] * lv[1:2], axis=-1, keepdims=True)
    t2 = jnp.sum(lv[2:3] * lv[3:4], axis=-1, keepdims=True)
    lam = jnp.exp(t1) - jnp.exp(t2) + lambda_init

    for g in range(NSTREAM):
        acc = acc_ref[g]
        o_all = acc[:BLK] / acc[BLK:BLK + 1]
        o_t = o_all[:, :TQ] - lam * o_all[:, TQ:]
        ms = jnp.mean(o_t * o_t, axis=0, keepdims=True)
        o = (o_t * lax.rsqrt(ms + EPS)).T * g_ref[...] * (1.0 - lambda_init)
        o_ref[0, :, cols(g)] = o.astype(o_ref.dtype)


def _attn_c(proj, vt, slopes, lam_vecs, subln_g, lambda_init, seq):
    bsz, lp, _ = proj.shape
    gw = NSTREAM * BLK
    kcol = C_QK // gw
    return pl.pallas_call(
        functools.partial(_attn_c_kernel, seq=seq, lambda_init=lambda_init),
        grid=(bsz, C_QK // gw, pl.cdiv(lp, TQ)),
        in_specs=[
            pl.BlockSpec(memory_space=pltpu.SMEM),
            pl.BlockSpec((4, HEAD_DIM), lambda b, p, i: (0, 0)),
            pl.BlockSpec((1, 2 * HEAD_DIM), lambda b, p, i: (0, 0)),
            pl.BlockSpec((1, TQ, gw), lambda b, p, i: (b, i, p)),
            pl.BlockSpec((1, lp, gw), lambda b, p, i: (b, 0, kcol + p)),
            pl.BlockSpec((1, gw, lp), lambda b, p, i: (b, p, 0)),
        ],
        out_specs=pl.BlockSpec((1, TQ, gw), lambda b, p, i: (b, i, p)),
        out_shape=jax.ShapeDtypeStruct((bsz, lp, C_QK), BF16),
        scratch_shapes=[pltpu.VMEM((NSTREAM, 1, 2 * TQ), F32),
                        pltpu.VMEM((NSTREAM, ACC_ROWS, 2 * TQ), F32)],
        compiler_params=_params(("arbitrary", "arbitrary", "arbitrary")),
        name="attn_c",
    )(slopes, lam_vecs, subln_g.reshape(1, 2 * HEAD_DIM), proj, proj, vt)


def _alibi_slopes(n):
    return jnp.exp2(-8.0 * (jnp.arange(n, dtype=F32) + 1.0) / n)


def kernel(x, meta_tokens, ab_norm, w_in_ab, attn_sinks, w_out_ab, c_norm, w_in_c,
           diff_lambda, diff_subln, w_out_c, mlp_norm, w_mlp_in, w_mlp_out, final_norm):
    bsz, seq, d = x.shape
    lp = seq + BLK
    assert seq % TQ == 0 and d == D_MODEL
    t = bsz * lp
    depth = mlp_norm.shape[0]

    meta = jnp.broadcast_to(meta_tokens.astype(x.dtype)[None], (bsz, N_META, d))
    filler = jnp.zeros((bsz, BLK - N_META, d), x.dtype)
    h = jnp.concatenate([x, meta, filler], axis=1).reshape(t, d)

    tm = lp // 2

    for layer in range(depth):
        li = layer // 2
        if layer % 2 == 0:
            proj = _norm_matmul(h, ab_norm[li], w_in_ab, li, relu2=False, tm=tm, tn=512)
            proj = proj.reshape(bsz, lp, AB_IN)
            vt_a = _transposed_cols(proj, A_Q + A_KV, A_KV)
            out_a = _attn_a(proj, vt_a, _alibi_slopes(A_HEADS), attn_sinks[li].astype(F32), seq)
            vt_b = _transposed_cols(proj, AB_IN - B_W, B_W)
            out_b = _attn_b(proj, vt_b, seq)
            h = _matmul_res([out_a.reshape(t, A_Q), out_b.reshape(t, B_W)], w_out_ab, li, h,
                            tm=lp, tn=512, tk=A_Q + B_W)
        else:
            lambda_init = 0.8 - 0.6 * math.exp(-0.3 * layer)
            proj = _norm_matmul(h, c_norm[li], w_in_c, li, relu2=False, tm=tm, tn=1024)
            proj = proj.reshape(bsz, lp, 3 * C_QK)
            vt_c = _transposed_cols(proj, 2 * C_QK, C_QK)
            out_c = _attn_c(proj, vt_c, _alibi_slopes(C_HEADS), diff_lambda[li].astype(F32),
                            diff_subln[li].astype(F32), lambda_init, seq)
            h = _matmul_res([out_c.reshape(t, C_QK)], w_out_c, li, h, tm=lp, tn=512, tk=2048)
        u = _norm_matmul(h, mlp_norm[layer], w_mlp_in, layer, relu2=True, tm=tm, tn=1024)
        h = _matmul_res([u], w_mlp_out, layer, h, tm=tm, tn=1024, tk=2048)

    return _final_norm(h.reshape(bsz, lp, d), final_norm, seq)
```

```python
import functools
import math

import jax
import jax.numpy as jnp
from jax import lax
from jax.experimental import pallas as pl
from jax.experimental.pallas import tpu as pltpu

D_MODEL = 2048
N_META = 16
CHUNK = 64
HEAD_DIM = 64
EPS = 1e-6
NEG = -1e30
HIDE = 1e32

A_HEADS = 16
A_KV_HEADS = 4
A_GROUP = A_HEADS // A_KV_HEADS
WINDOW_CHUNKS = 2
B_HEADS = 16
C_HEADS = D_MODEL // (2 * HEAD_DIM)
D_FF = 4 * D_MODEL

A_Q = A_HEADS * HEAD_DIM
A_KV = A_KV_HEADS * HEAD_DIM
B_W = B_HEADS * HEAD_DIM
AB_IN = A_Q + 2 * A_KV + 3 * B_W
C_QK = C_HEADS * 2 * HEAD_DIM

BLK = 128
TQ = 256
NSTREAM = 8
B_STREAMS = 8
B_PART = 4
ACC_ROWS = BLK + 16
SCALE = HEAD_DIM ** -0.5

VMEM_LIMIT = 56 * 1024 * 1024

F32 = jnp.float32
BF16 = jnp.bfloat16


def _params(sem, vmem=VMEM_LIMIT):
    return pltpu.CompilerParams(dimension_semantics=sem, vmem_limit_bytes=vmem)


def _nt_dot(a, b):
    return lax.dot_general(a, b, (((1,), (1,)), ((), ())), preferred_element_type=F32)


def _split_heads_rows(x):
    lane = lax.broadcasted_iota(jnp.int32, x.shape, 1)
    zero = jnp.zeros_like(x)
    return jnp.concatenate([jnp.where(lane < HEAD_DIM, x, zero),
                            jnp.where(lane >= HEAD_DIM, x, zero)], axis=0)


def _norm_rows(x, g):
    ms = jnp.mean(x * x, axis=-1, keepdims=True)
    return (x * lax.rsqrt(ms + EPS) * g).astype(BF16)


def _norm_matmul_kernel(*refs, relu2, split):
    x_ref, tail_ref = (refs[0], refs[1]) if split else (refs[0], None)
    g_ref, w_ref, o_ref, hn_ref = refs[-4:]

    @pl.when(pl.program_id(1) == 0)
    def _():
        hn_ref[...] = _norm_rows(x_ref[0] if split else x_ref[...], g_ref[...])
        if split:
            @pl.when(pl.program_id(0) % 2 == 1)
            def _():
                hn_ref[hn_ref.shape[0] - BLK:, :] = _norm_rows(tail_ref[...], g_ref[...])

    acc = jnp.dot(hn_ref[...], w_ref[...].astype(BF16), preferred_element_type=F32)
    if relu2:
        acc = jnp.square(jnp.maximum(acc, 0.0))
    o_ref[...] = acc.astype(o_ref.dtype)


def _norm_matmul(x, g, w_stack, li, *, relu2, tm, tn):
    split = isinstance(x, tuple)
    if split:
        frames, tail = x
        bsz, seq, k = frames.shape
        t = bsz * (seq + BLK)
        assert 2 * tm == seq + BLK
        x_args = [frames, tail]
        x_specs = [pl.BlockSpec((1, tm, k), lambda i, j: (i // 2, i % 2, 0)),
                   pl.BlockSpec((BLK, k), lambda i, j: (0, 0))]
    else:
        t, k = x.shape
        x_args = [x]
        x_specs = [pl.BlockSpec((tm, k), lambda i, j: (i, 0))]
    n = w_stack.shape[2]
    assert t % tm == 0 and n % tn == 0
    return pl.pallas_call(
        functools.partial(_norm_matmul_kernel, relu2=relu2, split=split),
        grid=(t // tm, n // tn),
        in_specs=x_specs + [
            pl.BlockSpec((1, k), lambda i, j: (0, 0)),
            pl.BlockSpec((None, k, tn), lambda i, j: (li, 0, j)),
        ],
        out_specs=pl.BlockSpec((tm, tn), lambda i, j: (i, j)),
        out_shape=jax.ShapeDtypeStruct((t, n), BF16),
        scratch_shapes=[pltpu.VMEM((tm, k), BF16)],
        compiler_params=_params(("arbitrary", "arbitrary")),
        name="norm_matmul_relu2" if relu2 else "norm_matmul",
    )(*x_args, g.reshape(1, k), w_stack)


def _matmul_res_kernel(*refs, n_a, split):
    a_refs, w_ref, r_refs, o_ref = refs[:n_a], refs[n_a], refs[n_a + 1:-1], refs[-1]
    w = w_ref[...].astype(BF16)
    ka = w.shape[0] // n_a
    part = jnp.dot(a_refs[0][...], w[:ka], preferred_element_type=F32)
    for s in range(1, n_a):
        part += jnp.dot(a_refs[s][...], w[s * ka:(s + 1) * ka], preferred_element_type=F32)

    @pl.when(pl.program_id(2) == 0)
    def _():
        if split:
            res = jnp.concatenate([r_refs[0][0], r_refs[1][...]], axis=0)
        else:
            res = r_refs[0][...]
        o_ref[...] = res + part

    @pl.when(pl.program_id(2) != 0)
    def _():
        o_ref[...] += part


def _matmul_res(a_parts, w_stack, li, res, *, tm, tn, tk):
    n_a = len(a_parts)
    t = a_parts[0].shape[0]
    _, k, n = w_stack.shape
    split = isinstance(res, tuple)
    if split:
        assert tm == res[0].shape[1] + BLK
        r_args = list(res)
        r_specs = [pl.BlockSpec((1, tm - BLK, tn), lambda i, j, kk: (i, 0, j)),
                   pl.BlockSpec((BLK, tn), lambda i, j, kk: (0, j))]
    else:
        r_args = [res]
        r_specs = [pl.BlockSpec((tm, tn), lambda i, j, kk: (i, j))]
    assert sum(a.shape[1] for a in a_parts) == k and (n_a == 1 or tk == k)
    assert t % tm == 0 and n % tn == 0 and k % tk == 0
    return pl.pallas_call(
        functools.partial(_matmul_res_kernel, n_a=n_a, split=split),
        grid=(t // tm, n // tn, k // tk),
        in_specs=[pl.BlockSpec((tm, tk // n_a), lambda i, j, kk: (i, kk)) for _ in a_parts] + [
            pl.BlockSpec((None, tk, tn), lambda i, j, kk: (li, kk, j))] + r_specs,
        out_specs=pl.BlockSpec((tm, tn), lambda i, j, kk: (i, j)),
        out_shape=jax.ShapeDtypeStruct((t, n), F32),
        compiler_params=_params(("arbitrary", "arbitrary", "arbitrary")),
        name="matmul_res",
    )(*a_parts, w_stack, *r_args)


def _final_norm_kernel(x_ref, g_ref, o_ref):
    x = x_ref[0]
    ms = jnp.mean(x * x, axis=-1, keepdims=True)
    o_ref[0] = x * lax.rsqrt(ms + EPS) * g_ref[...]


def _final_norm(h3, g, seq):
    bsz, lp, d = h3.shape
    rows = 2 * BLK
    return pl.pallas_call(
        _final_norm_kernel,
        grid=(bsz, seq // rows),
        in_specs=[
            pl.BlockSpec((1, rows, d), lambda b, r: (b, r, 0)),
            pl.BlockSpec((1, d), lambda b, r: (0, 0)),
        ],
        out_specs=pl.BlockSpec((1, rows, d), lambda b, r: (b, r, 0)),
        out_shape=jax.ShapeDtypeStruct((bsz, seq, d), F32),
        compiler_params=_params(("arbitrary", "arbitrary")),
        name="final_norm",
    )(h3, g.reshape(1, d))


def _transpose_kernel(x_ref, o_ref):
    for c in range(x_ref.shape[2] // BLK):
        o_ref[0, c * BLK:(c + 1) * BLK, :] = x_ref[0, :, c * BLK:(c + 1) * BLK].T


def _transposed_cols(proj, col0, ncols):
    bsz, lp, _ = proj.shape
    cw = math.gcd(col0, ncols, 8 * BLK)
    assert cw % BLK == 0
    return pl.pallas_call(
        _transpose_kernel,
        grid=(bsz, ncols // cw),
        in_specs=[pl.BlockSpec((1, lp, cw), lambda b, c: (b, 0, col0 // cw + c))],
        out_specs=pl.BlockSpec((1, cw, lp), lambda b, c: (b, c, 0)),
        out_shape=jax.ShapeDtypeStruct((bsz, ncols, lp), proj.dtype),
        compiler_params=_params(("arbitrary", "arbitrary")),
        name="transpose_v",
    )(proj)


A_BAND = TQ + BLK
A_VROWS = HEAD_DIM + 16


def _attn_a_kernel(slopes_ref, sinks_ref, q_ref, k_ref, vt_ref, o_ref, *, seq):
    i = pl.program_id(1)
    lp = k_ref.shape[1]
    wl = A_GROUP * TQ
    start = pl.multiple_of(jnp.clip(i * TQ - BLK, 0, lp - A_BAND), BLK)
    band = pl.ds(start, A_BAND)
    meta = pl.ds(seq, N_META)

    def positions(nrows):
        r = lax.broadcasted_iota(jnp.int32, (nrows, TQ), 0)
        u_q = i * TQ + lax.broadcasted_iota(jnp.int32, (nrows, TQ), 1)
        q_frame = u_q < seq
        return r, jnp.where(q_frame, u_q + N_META, u_q - seq), jnp.where(q_frame, 1 + (u_q >> 6), 0)

    r, pos_q, qc = positions(A_BAND)
    u_k = start + r
    kc = 1 + (u_k >> 6)
    vis = (u_k < seq) & (kc <= qc) & (kc >= qc - WINDOW_CHUNKS)
    dist = jnp.where(vis, jnp.abs(pos_q - (u_k + N_META)).astype(F32), HIDE)
    rm, pos_qm, _ = positions(N_META)
    dist_m = jnp.abs(pos_qm - rm).astype(F32)
    head_lane = lax.broadcasted_iota(jnp.int32, (1, wl), 1) >> 8

    def per_head(ref, g):
        v = jnp.zeros((1, wl), F32)
        for a in range(A_GROUP):
            v = jnp.where(head_lane == a, ref[g * A_GROUP + a], v)
        return v

    def dup(kt, g):
        kg = kt[:, g * HEAD_DIM:(g + 1) * HEAD_DIM]
        return jnp.concatenate([kg, kg], axis=1)

    groups = range(A_KV_HEADS)
    k_band, k_meta = k_ref[0, band, :], k_ref[0, meta, :]
    qd = [jnp.concatenate(
        [_split_heads_rows(q_ref[0, :, (2 * g + p) * BLK:(2 * g + p + 1) * BLK] * SCALE)
         for p in range(2)], axis=0) for g in groups]
    s_b = [_nt_dot(dup(k_band, g), qd[g]) for g in groups]
    s_m = [_nt_dot(dup(k_meta, g), qd[g]) for g in groups]
    e_b, e_m, extra = [], [], []
    for g in groups:
        slope, sink = per_head(slopes_ref, g), per_head(sinks_ref, g)
        t_b = s_b[g] - slope * jnp.concatenate([dist] * A_GROUP, axis=1)
        t_m = s_m[g] - slope * jnp.concatenate([dist_m] * A_GROUP, axis=1)
        mx = jnp.maximum(jnp.maximum(jnp.max(t_b, axis=0, keepdims=True),
                                     jnp.max(t_m, axis=0, keepdims=True)), sink)
        e_b.append(jnp.exp(t_b - mx).astype(BF16))
        e_m.append(jnp.exp(t_m - mx).astype(BF16))
        extra.append(jnp.exp(sink - mx))

    def vt_ones(g, rows):
        return jnp.concatenate([vt_ref[0, g * HEAD_DIM:(g + 1) * HEAD_DIM, rows],
                                jnp.ones((A_VROWS - HEAD_DIM, rows.size), BF16)], axis=0)

    acc = [jnp.dot(vt_ones(g, band), e_b[g], preferred_element_type=F32)
           + jnp.dot(vt_ones(g, meta), e_m[g], preferred_element_type=F32) for g in groups]
    for g in groups:
        o_t = acc[g][:HEAD_DIM] / (acc[g][HEAD_DIM:HEAD_DIM + 1] + extra[g])
        for p in range(2):
            pair = jnp.concatenate([o_t[:, (2 * p) * TQ:(2 * p + 1) * TQ],
                                    o_t[:, (2 * p + 1) * TQ:(2 * p + 2) * TQ]], axis=0)
            o_ref[0, :, (2 * g + p) * BLK:(2 * g + p + 1) * BLK] = pair.T.astype(o_ref.dtype)


def _attn_a(proj, vt, slopes, sinks, seq):
    bsz, lp, _ = proj.shape
    kcol = A_Q // A_KV
    return pl.pallas_call(
        functools.partial(_attn_a_kernel, seq=seq),
        grid=(bsz, pl.cdiv(lp, TQ)),
        in_specs=[
            pl.BlockSpec(memory_space=pltpu.SMEM),
            pl.BlockSpec(memory_space=pltpu.SMEM),
            pl.BlockSpec((1, TQ, A_Q), lambda b, i: (b, i, 0)),
            pl.BlockSpec((1, lp, A_KV), lambda b, i: (b, 0, kcol)),
            pl.BlockSpec((1, A_KV, lp), lambda b, i: (b, 0, 0)),
        ],
        out_specs=pl.BlockSpec((1, TQ, A_Q), lambda b, i: (b, i, 0)),
        out_shape=jax.ShapeDtypeStruct((bsz, lp, A_Q), BF16),
        compiler_params=_params(("arbitrary", "arbitrary")),
        name="attn_a",
    )(slopes, sinks, proj, proj, vt)


def _attn_b_kernel(*refs, seq):
    n_part = B_STREAMS // B_PART
    q_refs, k_refs = refs[:n_part], refs[n_part:2 * n_part]
    vt_ref, o_ref, run_ref, acc_ref = refs[2 * n_part:]
    i = pl.program_id(2)
    n_ft = seq // TQ
    w = 2 * TQ

    def part(part_refs, g, rows):
        lo = (g % B_PART) * BLK
        return part_refs[g // B_PART][0, rows, lo:lo + BLK]

    streams = range(B_STREAMS)
    qd = [_split_heads_rows(part(q_refs, g, slice(None)) * SCALE)
          for g in streams]
    run_ref[...] = jnp.zeros_like(run_ref)
    acc_ref[...] = jnp.zeros_like(acc_ref)

    def tri(tk):
        s_idx = lax.broadcasted_iota(jnp.int32, (tk + 16, tk), 0)
        j_idx = lax.broadcasted_iota(jnp.int32, (tk + 16, tk), 1)
        return jnp.where((j_idx > s_idx) | (s_idx >= tk), 1.0, 0.0).astype(BF16)

    def tile(rows, tri_op, mask):
        zs = [_nt_dot(part(k_refs, g, rows), qd[g]) for g in streams]
        tk = rows.size
        sps, newers = [], []
        for g in streams:
            sp = jnp.maximum(zs[g], 0.0) + jnp.log(1.0 + jnp.exp(-jnp.abs(zs[g])))
            if mask is not None:
                sp = jnp.where(mask, sp, 0.0)
            newers.append(jnp.dot(tri_op, sp.astype(BF16), preferred_element_type=F32))
            sps.append(sp)
        pvs = []
        for g in streams:
            wgt = jnp.exp(zs[g] - (sps[g] + newers[g][:tk] + run_ref[g]))
            if mask is not None:
                wgt = jnp.where(mask, wgt, 0.0)
            pvs.append(jnp.dot(vt_ref[0, g * BLK:(g + 1) * BLK, rows], wgt.astype(BF16),
                               preferred_element_type=F32))
        for g in streams:
            acc_ref[g] += pvs[g]
            run_ref[g] += newers[g][tk:tk + 1]

    @pl.when(i < n_ft)
    def _():
        tri_tq = tri(TQ)
        r = lax.broadcasted_iota(jnp.int32, (TQ, w), 0)
        c = lax.broadcasted_iota(jnp.int32, (TQ, w), 1) & (TQ - 1)
        tile(pl.ds(pl.multiple_of(i * TQ, TQ), TQ), tri_tq, r < c)

        def body(jj, carry):
            tile(pl.ds(pl.multiple_of((i - 1 - jj) * TQ, TQ), TQ), tri_tq, None)
            return carry

        lax.fori_loop(0, i, body, 0)

    r = lax.broadcasted_iota(jnp.int32, (N_META, w), 0)
    c = lax.broadcasted_iota(jnp.int32, (N_META, w), 1) & (TQ - 1)
    tile(pl.ds(seq, N_META), tri(N_META), (i < n_ft) | (r < c))

    for g in streams:
        acc = acc_ref[g]
        o_t = jnp.concatenate([acc[:HEAD_DIM, :TQ], acc[HEAD_DIM:, TQ:]], axis=0)
        o_ref[0, :, g * BLK:(g + 1) * BLK] = o_t.T.astype(o_ref.dtype)


def _attn_b(proj, vt, seq):
    bsz, lp, _ = proj.shape
    gw = B_STREAMS * BLK
    pw = B_PART * BLK
    n_part = B_STREAMS // B_PART
    assert (A_Q + 2 * A_KV) % pw == 0 and B_W % gw == 0 and B_STREAMS % B_PART == 0
    qcol = (A_Q + 2 * A_KV) // pw
    kcol = qcol + B_W // pw

    def col_specs(rows, first, row_index):
        return [pl.BlockSpec((1, rows, pw),
                             lambda b, p, i, s=s: (b, row_index(i), first + p * n_part + s))
                for s in range(n_part)]

    return pl.pallas_call(
        functools.partial(_attn_b_kernel, seq=seq),
        grid=(bsz, B_W // gw, pl.cdiv(lp, TQ)),
        in_specs=col_specs(TQ, qcol, lambda i: i) + col_specs(lp, kcol, lambda i: 0) + [
            pl.BlockSpec((1, gw, lp), lambda b, p, i: (b, p, 0)),
        ],
        out_specs=pl.BlockSpec((1, TQ, gw), lambda b, p, i: (b, i, p)),
        out_shape=jax.ShapeDtypeStruct((bsz, lp, B_W), BF16),
        scratch_shapes=[pltpu.VMEM((B_STREAMS, 1, 2 * TQ), F32),
                        pltpu.VMEM((B_STREAMS, BLK, 2 * TQ), F32)],
        compiler_params=_params(("arbitrary", "arbitrary", "arbitrary")),
        name="attn_b",
    )(*([proj] * (2 * n_part)), vt)


def _attn_c_kernel(slopes_ref, lam_ref, g_ref, q_ref, k_ref, vt_ref, o_ref,
                   m_ref, acc_ref, *, seq, lambda_init):
    p = pl.program_id(1)
    i = pl.program_id(2)
    n_ft = seq // TQ
    w = 2 * TQ
    slope = [slopes_ref[p * NSTREAM + g] for g in range(NSTREAM)]
    qd = [_split_heads_rows(q_ref[0, :, g * BLK:(g + 1) * BLK] * SCALE)
          for g in range(NSTREAM)]
    meta = pl.ds(seq, N_META)

    def cols(g):
        return slice(g * BLK, (g + 1) * BLK)

    def vt_ones(g, rows):
        tk = rows.size
        return jnp.concatenate([vt_ref[0, cols(g), rows], jnp.ones((ACC_ROWS - BLK, tk), BF16)],
                               axis=0)

    def meta_scores(g):
        return _nt_dot(k_ref[0, meta, cols(g)], qd[g])

    rm = lax.broadcasted_iota(jnp.int32, (N_META, w), 0)
    cm = lax.broadcasted_iota(jnp.int32, (N_META, w), 1) & (TQ - 1)

    @pl.when(i == n_ft)
    def _():
        dist = jnp.abs(cm - rm).astype(F32)
        for g in range(NSTREAM):
            t = meta_scores(g) - slope[g] * dist
            m0 = jnp.max(t, axis=0, keepdims=True)
            e0 = jnp.exp(t - m0)
            m_ref[g] = m0
            acc_ref[g] = jnp.dot(vt_ones(g, meta), e0.astype(BF16), preferred_element_type=F32)

    @pl.when(i < n_ft)
    def _():
        r = lax.broadcasted_iota(jnp.int32, (TQ, w), 0)
        c = lax.broadcasted_iota(jnp.int32, (TQ, w), 1) & (TQ - 1)
        lane_k = lax.broadcasted_iota(jnp.int32, (TQ, BLK), 1)
        row_k = lax.broadcasted_iota(jnp.int32, (TQ, BLK), 0)
        key_off = jnp.where(lane_k < 3, row_k, 0).astype(F32).astype(BF16)
        lane_q = lax.broadcasted_iota(jnp.int32, (16, BLK), 1)
        qa = []
        for g in range(NSTREAM):
            sl = jnp.full((16, BLK), slope[g], F32)
            s_hi = sl.astype(BF16).astype(F32)
            s_mid = (sl - s_hi).astype(BF16).astype(F32)
            s_lo = (sl - s_hi) - s_mid
            sl3 = jnp.where(lane_q == 0, s_hi,
                            jnp.where(lane_q == 1, s_mid, jnp.where(lane_q == 2, s_lo, 0.0)))
            sl3 = jnp.concatenate([sl3.astype(BF16)] * (w // 16), axis=0)
            qa.append(jnp.concatenate([qd[g], sl3], axis=1))

        def scores(g, rows):
            return _nt_dot(jnp.concatenate([k_ref[0, rows, cols(g)], key_off], axis=1), qa[g])

        diag = pl.ds(pl.multiple_of(i * TQ, TQ), TQ)
        vis = (r >> 6) <= (c >> 6)
        ahead = jnp.where(vis, 2.0 * jnp.maximum(r - c, 0).astype(F32), HIDE)
        meta_off = (rm - (N_META + i * TQ)).astype(F32)
        s_d = [scores(g, diag) for g in range(NSTREAM)]
        s_m = [meta_scores(g) for g in range(NSTREAM)]
        e_d, e_m = [], []
        for g in range(NSTREAM):
            t_d = s_d[g] - slope[g] * ahead
            t_m = s_m[g] + slope[g] * meta_off
            m0 = jnp.maximum(jnp.max(t_d, axis=0, keepdims=True),
                             jnp.max(t_m, axis=0, keepdims=True))
            e_d.append(jnp.exp(t_d - m0).astype(BF16))
            e_m.append(jnp.exp(t_m - m0).astype(BF16))
            m_ref[g] = m0
        for g in range(NSTREAM):
            acc_ref[g] = (jnp.dot(vt_ones(g, diag), e_d[g], preferred_element_type=F32)
                          + jnp.dot(vt_ones(g, meta), e_m[g], preferred_element_type=F32))

        def older_tile(j):
            rows = pl.ds(pl.multiple_of(j * TQ, TQ), TQ)
            s = [scores(g, rows) for g in range(NSTREAM)]
            es, corrs = [], []
            for g in range(NSTREAM):
                cj = slope[g] * ((i - j) * TQ).astype(F32)
                m_old = m_ref[g]
                m_new = jnp.maximum(m_old, jnp.max(s[g], axis=0, keepdims=True) - cj)
                es.append(jnp.exp(s[g] - (m_new + cj)).astype(BF16))
                corrs.append(jnp.exp(m_old - m_new))
                m_ref[g] = m_new
            pvs = [jnp.dot(vt_ones(g, rows), es[g], preferred_element_type=F32)
                   for g in range(NSTREAM)]
            for g in range(NSTREAM):
                acc_ref[g] = corrs[g] * acc_ref[g] + pvs[g]

        def body(j, carry):
            older_tile(j)
            return carry

        lax.fori_loop(0, i, body, 0)

    lv = lam_ref[...]
    t1 = jnp.sum(lv[0:1] * lv[1:2], axis=-1, keepdims=True)
    t2 = jnp.sum(lv[2:3] * lv[3:4], axis=-1, keepdims=True)
    lam = jnp.exp(t1) - jnp.exp(t2) + lambda_init

    for g in range(NSTREAM):
        acc = acc_ref[g]
        o_all = acc[:BLK] / acc[BLK:BLK + 1]
        o_t = o_all[:, :TQ] - lam * o_all[:, TQ:]
        ms = jnp.mean(o_t * o_t, axis=0, keepdims=True)
        o = (o_t * lax.rsqrt(ms + EPS)).T * g_ref[...] * (1.0 - lambda_init)
        o_ref[0, :, cols(g)] = o.astype(o_ref.dtype)


def _attn_c(proj, vt, slopes, lam_vecs, subln_g, lambda_init, seq):
    bsz, lp, _ = proj.shape
    gw = NSTREAM * BLK
    kcol = C_QK // gw
    return pl.pallas_call(
        functools.partial(_attn_c_kernel, seq=seq, lambda_init=lambda_init),
        grid=(bsz, C_QK // gw, pl.cdiv(lp, TQ)),
        in_specs=[
            pl.BlockSpec(memory_space=pltpu.SMEM),
            pl.BlockSpec((4, HEAD_DIM), lambda b, p, i: (0, 0)),
            pl.BlockSpec((1, 2 * HEAD_DIM), lambda b, p, i: (0, 0)),
            pl.BlockSpec((1, TQ, gw), lambda b, p, i: (b, i, p)),
            pl.BlockSpec((1, lp, gw), lambda b, p, i: (b, 0, kcol + p)),
            pl.BlockSpec((1, gw, lp), lambda b, p, i: (b, p, 0)),
        ],
        out_specs=pl.BlockSpec((1, TQ, gw), lambda b, p, i: (b, i, p)),
        out_shape=jax.ShapeDtypeStruct((bsz, lp, C_QK), BF16),
        scratch_shapes=[pltpu.VMEM((NSTREAM, 1, 2 * TQ), F32),
                        pltpu.VMEM((NSTREAM, ACC_ROWS, 2 * TQ), F32)],
        compiler_params=_params(("arbitrary", "arbitrary", "arbitrary")),
        name="attn_c",
    )(slopes, lam_vecs, subln_g.reshape(1, 2 * HEAD_DIM), proj, proj, vt)


def _alibi_slopes(n):
    return jnp.exp2(-8.0 * (jnp.arange(n, dtype=F32) + 1.0) / n)


def kernel(x, meta_tokens, ab_norm, w_in_ab, attn_sinks, w_out_ab, c_norm, w_in_c,
           diff_lambda, diff_subln, w_out_c, mlp_norm, w_mlp_in, w_mlp_out, final_norm):
    bsz, seq, d = x.shape
    lp = seq + BLK
    assert seq % TQ == 0 and d == D_MODEL
    t = bsz * lp
    depth = mlp_norm.shape[0]

    tail = jnp.concatenate([meta_tokens.astype(x.dtype), jnp.zeros((BLK - N_META, d), x.dtype)])
    h = (x, tail)

    tm = lp // 2

    for layer in range(depth):
        li = layer // 2
        if layer % 2 == 0:
            proj = _norm_matmul(h, ab_norm[li], w_in_ab, li, relu2=False, tm=tm, tn=512)
            proj = proj.reshape(bsz, lp, AB_IN)
            vt_a = _transposed_cols(proj, A_Q + A_KV, A_KV)
            out_a = _attn_a(proj, vt_a, _alibi_slopes(A_HEADS), attn_sinks[li].astype(F32), seq)
            vt_b = _transposed_cols(proj, AB_IN - B_W, B_W)
            out_b = _attn_b(proj, vt_b, seq)
            h = _matmul_res([out_a.reshape(t, A_Q), out_b.reshape(t, B_W)], w_out_ab, li, h,
                            tm=lp, tn=512, tk=A_Q + B_W)
        else:
            lambda_init = 0.8 - 0.6 * math.exp(-0.3 * layer)
            proj = _norm_matmul(h, c_norm[li], w_in_c, li, relu2=False, tm=tm, tn=1024)
            proj = proj.reshape(bsz, lp, 3 * C_QK)
            vt_c = _transposed_cols(proj, 2 * C_QK, C_QK)
            out_c = _attn_c(proj, vt_c, _alibi_slopes(C_HEADS), diff_lambda[li].astype(F32),
                            diff_subln[li].astype(F32), lambda_init, seq)
            h = _matmul_res([out_c.reshape(t, C_QK)], w_out_c, li, h, tm=lp, tn=512, tk=2048)
        u = _norm_matmul(h, mlp_norm[layer], w_mlp_in, layer, relu2=True, tm=tm, tn=1024)
        h = _matmul_res([u], w_mlp_out, layer, h, tm=tm, tn=1024, tk=2048)

    return _final_norm(h.reshape(bsz, lp, d), final_norm, seq)
```

```python
import functools
import math

import jax
import jax.numpy as jnp
from jax import lax
from jax.experimental import pallas as pl
from jax.experimental.pallas import tpu as pltpu

D_MODEL = 2048
N_META = 16
CHUNK = 64
HEAD_DIM = 64
EPS = 1e-6
NEG = -1e30
HIDE = 1e32

A_HEADS = 16
A_KV_HEADS = 4
A_GROUP = A_HEADS // A_KV_HEADS
WINDOW_CHUNKS = 2
B_HEADS = 16
C_HEADS = D_MODEL // (2 * HEAD_DIM)
D_FF = 4 * D_MODEL

A_Q = A_HEADS * HEAD_DIM
A_KV = A_KV_HEADS * HEAD_DIM
B_W = B_HEADS * HEAD_DIM
AB_IN = A_Q + 2 * A_KV + 3 * B_W
C_QK = C_HEADS * 2 * HEAD_DIM

BLK = 128
TQ = 256
NSTREAM = 8
B_STREAMS = 8
B_PART = 4
ACC_ROWS = BLK + 16
SCALE = HEAD_DIM ** -0.5

VMEM_LIMIT = 56 * 1024 * 1024

F32 = jnp.float32
BF16 = jnp.bfloat16


def _params(sem, vmem=VMEM_LIMIT):
    return pltpu.CompilerParams(dimension_semantics=sem, vmem_limit_bytes=vmem)


def _nt_dot(a, b):
    return lax.dot_general(a, b, (((1,), (1,)), ((), ())), preferred_element_type=F32)


def _split_heads_rows(x):
    lane = lax.broadcasted_iota(jnp.int32, x.shape, 1)
    zero = jnp.zeros_like(x)
    return jnp.concatenate([jnp.where(lane < HEAD_DIM, x, zero),
                            jnp.where(lane >= HEAD_DIM, x, zero)], axis=0)


def _norm_rows(x, g):
    ms = jnp.mean(x * x, axis=-1, keepdims=True)
    return (x * lax.rsqrt(ms + EPS) * g).astype(BF16)


def _norm_matmul_kernel(*refs, relu2, split):
    x_ref, tail_ref = (refs[0], refs[1]) if split else (refs[0], None)
    g_ref, w_ref, o_ref, hn_ref = refs[-4:]

    @pl.when(pl.program_id(1) == 0)
    def _():
        hn_ref[...] = _norm_rows(x_ref[0] if split else x_ref[...], g_ref[...])
        if split:
            @pl.when(pl.program_id(0) % 2 == 1)
            def _():
                hn_ref[hn_ref.shape[0] - BLK:, :] = _norm_rows(tail_ref[...], g_ref[...])

    acc = jnp.dot(hn_ref[...], w_ref[...].astype(BF16), preferred_element_type=F32)
    if relu2:
        acc = jnp.square(jnp.maximum(acc, 0.0))
    o_ref[...] = acc.astype(o_ref.dtype)


def _norm_matmul(x, g, w_stack, li, *, relu2, tm, tn):
    split = isinstance(x, tuple)
    if split:
        frames, tail = x
        bsz, seq, k = frames.shape
        t = bsz * (seq + BLK)
        assert 2 * tm == seq + BLK
        x_args = [frames, tail]
        x_specs = [pl.BlockSpec((1, tm, k), lambda i, j: (i // 2, i % 2, 0)),
                   pl.BlockSpec((BLK, k), lambda i, j: (0, 0))]
    else:
        t, k = x.shape
        x_args = [x]
        x_specs = [pl.BlockSpec((tm, k), lambda i, j: (i, 0))]
    n = w_stack.shape[2]
    assert t % tm == 0 and n % tn == 0
    return pl.pallas_call(
        functools.partial(_norm_matmul_kernel, relu2=relu2, split=split),
        grid=(t // tm, n // tn),
        in_specs=x_specs + [
            pl.BlockSpec((1, k), lambda i, j: (0, 0)),
            pl.BlockSpec((None, k, tn), lambda i, j: (li, 0, j)),
        ],
        out_specs=pl.BlockSpec((tm, tn), lambda i, j: (i, j)),
        out_shape=jax.ShapeDtypeStruct((t, n), BF16),
        scratch_shapes=[pltpu.VMEM((tm, k), BF16)],
        compiler_params=_params(("arbitrary", "arbitrary")),
        name="norm_matmul_relu2" if relu2 else "norm_matmul",
    )(*x_args, g.reshape(1, k), w_stack)


def _matmul_res_kernel(*refs, n_a, split):
    a_refs, w_ref, r_refs, o_ref = refs[:n_a], refs[n_a], refs[n_a + 1:-1], refs[-1]
    w = w_ref[...].astype(BF16)
    ka = w.shape[0] // n_a
    part = jnp.dot(a_refs[0][...], w[:ka], preferred_element_type=F32)
    for s in range(1, n_a):
        part += jnp.dot(a_refs[s][...], w[s * ka:(s + 1) * ka], preferred_element_type=F32)

    @pl.when(pl.program_id(2) == 0)
    def _():
        if split:
            res = jnp.concatenate([r_refs[0][0], r_refs[1][...]], axis=0)
        else:
            res = r_refs[0][...]
        o_ref[...] = res + part

    @pl.when(pl.program_id(2) != 0)
    def _():
        o_ref[...] += part


def _matmul_res(a_parts, w_stack, li, res, *, tm, tn, tk):
    n_a = len(a_parts)
    t = a_parts[0].shape[0]
    _, k, n = w_stack.shape
    split = isinstance(res, tuple)
    if split:
        assert tm == res[0].shape[1] + BLK
        r_args = list(res)
        r_specs = [pl.BlockSpec((1, tm - BLK, tn), lambda i, j, kk: (i, 0, j)),
                   pl.BlockSpec((BLK, tn), lambda i, j, kk: (0, j))]
    else:
        r_args = [res]
        r_specs = [pl.BlockSpec((tm, tn), lambda i, j, kk: (i, j))]
    assert sum(a.shape[1] for a in a_parts) == k and (n_a == 1 or tk == k)
    assert t % tm == 0 and n % tn == 0 and k % tk == 0
    return pl.pallas_call(
        functools.partial(_matmul_res_kernel, n_a=n_a, split=split),
        grid=(t // tm, n // tn, k // tk),
        in_specs=[pl.BlockSpec((tm, tk // n_a), lambda i, j, kk: (i, kk)) for _ in a_parts] + [
            pl.BlockSpec((None, tk, tn), lambda i, j, kk: (li, kk, j))] + r_specs,
        out_specs=pl.BlockSpec((tm, tn), lambda i, j, kk: (i, j)),
        out_shape=jax.ShapeDtypeStruct((t, n), F32),
        compiler_params=_params(("arbitrary", "arbitrary", "arbitrary")),
        name="matmul_res",
    )(*a_parts, w_stack, *r_args)


def _final_norm_kernel(x_ref, g_ref, o_ref):
    x = x_ref[0]
    ms = jnp.mean(x * x, axis=-1, keepdims=True)
    o_ref[0] = x * lax.rsqrt(ms + EPS) * g_ref[...]


def _final_norm(h3, g, seq):
    bsz, lp, d = h3.shape
    rows = 2 * BLK
    return pl.pallas_call(
        _final_norm_kernel,
        grid=(bsz, seq // rows),
        in_specs=[
            pl.BlockSpec((1, rows, d), lambda b, r: (b, r, 0)),
            pl.BlockSpec((1, d), lambda b, r: (0, 0)),
        ],
        out_specs=pl.BlockSpec((1, rows, d), lambda b, r: (b, r, 0)),
        out_shape=jax.ShapeDtypeStruct((bsz, seq, d), F32),
        compiler_params=_params(("arbitrary", "arbitrary")),
        name="final_norm",
    )(h3, g.reshape(1, d))


def _transpose_kernel(x_ref, o_ref):
    for c in range(x_ref.shape[2] // BLK):
        o_ref[0, c * BLK:(c + 1) * BLK, :] = x_ref[0, :, c * BLK:(c + 1) * BLK].T


def _transposed_cols(proj, col0, ncols):
    bsz, lp, _ = proj.shape
    cw = math.gcd(col0, ncols, 8 * BLK)
    assert cw % BLK == 0
    return pl.pallas_call(
        _transpose_kernel,
        grid=(bsz, ncols // cw),
        in_specs=[pl.BlockSpec((1, lp, cw), lambda b, c: (b, 0, col0 // cw + c))],
        out_specs=pl.BlockSpec((1, cw, lp), lambda b, c: (b, c, 0)),
        out_shape=jax.ShapeDtypeStruct((bsz, ncols, lp), proj.dtype),
        compiler_params=_params(("arbitrary", "arbitrary")),
        name="transpose_v",
    )(proj)


A_BAND = TQ + BLK
A_VROWS = HEAD_DIM + 16


def _attn_a_kernel(slopes_ref, sinks_ref, q_ref, k_ref, vt_ref, o_ref, *, seq):
    i = pl.program_id(1)
    lp = k_ref.shape[1]
    wl = A_GROUP * TQ
    start = pl.multiple_of(jnp.clip(i * TQ - BLK, 0, lp - A_BAND), BLK)
    band = pl.ds(start, A_BAND)
    meta = pl.ds(seq, N_META)

    def positions(nrows):
        r = lax.broadcasted_iota(jnp.int32, (nrows, TQ), 0)
        u_q = i * TQ + lax.broadcasted_iota(jnp.int32, (nrows, TQ), 1)
        q_frame = u_q < seq
        return r, jnp.where(q_frame, u_q + N_META, u_q - seq), jnp.where(q_frame, 1 + (u_q >> 6), 0)

    r, pos_q, qc = positions(A_BAND)
    u_k = start + r
    kc = 1 + (u_k >> 6)
    vis = (u_k < seq) & (kc <= qc) & (kc >= qc - WINDOW_CHUNKS)
    dist = jnp.where(vis, jnp.abs(pos_q - (u_k + N_META)).astype(F32), HIDE)
    rm, pos_qm, _ = positions(N_META)
    dist_m = jnp.abs(pos_qm - rm).astype(F32)
    head_lane = lax.broadcasted_iota(jnp.int32, (1, wl), 1) >> 8

    def per_head(ref, g):
        v = jnp.zeros((1, wl), F32)
        for a in range(A_GROUP):
            v = jnp.where(head_lane == a, ref[g * A_GROUP + a], v)
        return v

    def dup(kt, g):
        kg = kt[:, g * HEAD_DIM:(g + 1) * HEAD_DIM]
        return jnp.concatenate([kg, kg], axis=1)

    groups = range(A_KV_HEADS)
    k_band, k_meta = k_ref[0, band, :], k_ref[0, meta, :]
    qd = [jnp.concatenate(
        [_split_heads_rows(q_ref[0, :, (2 * g + p) * BLK:(2 * g + p + 1) * BLK] * SCALE)
         for p in range(2)], axis=0) for g in groups]
    s_b = [_nt_dot(dup(k_band, g), qd[g]) for g in groups]
    s_m = [_nt_dot(dup(k_meta, g), qd[g]) for g in groups]
    e_b, e_m, extra = [], [], []
    for g in groups:
        slope, sink = per_head(slopes_ref, g), per_head(sinks_ref, g)
        t_b = s_b[g] - slope * jnp.concatenate([dist] * A_GROUP, axis=1)
        t_m = s_m[g] - slope * jnp.concatenate([dist_m] * A_GROUP, axis=1)
        mx = jnp.maximum(jnp.maximum(jnp.max(t_b, axis=0, keepdims=True),
                                     jnp.max(t_m, axis=0, keepdims=True)), sink)
        e_b.append(jnp.exp(t_b - mx).astype(BF16))
        e_m.append(jnp.exp(t_m - mx).astype(BF16))
        extra.append(jnp.exp(sink - mx))

    def vt_ones(g, rows):
        return jnp.concatenate([vt_ref[0, g * HEAD_DIM:(g + 1) * HEAD_DIM, rows],
                                jnp.ones((A_VROWS - HEAD_DIM, rows.size), BF16)], axis=0)

    acc = [jnp.dot(vt_ones(g, band), e_b[g], preferred_element_type=F32)
           + jnp.dot(vt_ones(g, meta), e_m[g], preferred_element_type=F32) for g in groups]
    for g in groups:
        o_t = acc[g][:HEAD_DIM] / (acc[g][HEAD_DIM:HEAD_DIM + 1] + extra[g])
        for p in range(2):
            pair = jnp.concatenate([o_t[:, (2 * p) * TQ:(2 * p + 1) * TQ],
                                    o_t[:, (2 * p + 1) * TQ:(2 * p + 2) * TQ]], axis=0)
            o_ref[0, :, (2 * g + p) * BLK:(2 * g + p + 1) * BLK] = pair.T.astype(o_ref.dtype)


def _attn_a(proj, vt, slopes, sinks, seq):
    bsz, lp, _ = proj.shape
    kcol = A_Q // A_KV
    return pl.pallas_call(
        functools.partial(_attn_a_kernel, seq=seq),
        grid=(bsz, pl.cdiv(lp, TQ)),
        in_specs=[
            pl.BlockSpec(memory_space=pltpu.SMEM),
            pl.BlockSpec(memory_space=pltpu.SMEM),
            pl.BlockSpec((1, TQ, A_Q), lambda b, i: (b, i, 0)),
            pl.BlockSpec((1, lp, A_KV), lambda b, i: (b, 0, kcol)),
            pl.BlockSpec((1, A_KV, lp), lambda b, i: (b, 0, 0)),
        ],
        out_specs=pl.BlockSpec((1, TQ, A_Q), lambda b, i: (b, i, 0)),
        out_shape=jax.ShapeDtypeStruct((bsz, lp, A_Q), BF16),
        compiler_params=_params(("arbitrary", "arbitrary")),
        name="attn_a",
    )(slopes, sinks, proj, proj, vt)


def _attn_b_kernel(*refs, seq):
    n_part = B_STREAMS // B_PART
    q_refs, k_refs = refs[:n_part], refs[n_part:2 * n_part]
    vt_ref, o_ref, run_ref, acc_ref = refs[2 * n_part:]
    i = pl.program_id(2)
    n_ft = seq // TQ
    w = 2 * TQ

    def part(part_refs, g, rows):
        lo = (g % B_PART) * BLK
        return part_refs[g // B_PART][0, rows, lo:lo + BLK]

    streams = range(B_STREAMS)
    qd = [_split_heads_rows(part(q_refs, g, slice(None)) * SCALE)
          for g in streams]
    run_ref[...] = jnp.zeros_like(run_ref)
    acc_ref[...] = jnp.zeros_like(acc_ref)

    def tri(tk):
        s_idx = lax.broadcasted_iota(jnp.int32, (tk + 16, tk), 0)
        j_idx = lax.broadcasted_iota(jnp.int32, (tk + 16, tk), 1)
        return jnp.where((j_idx > s_idx) | (s_idx >= tk), 1.0, 0.0).astype(BF16)

    def tile(rows, tri_op, mask):
        zs = [_nt_dot(part(k_refs, g, rows), qd[g]) for g in streams]
        tk = rows.size
        sps, newers = [], []
        for g in streams:
            sp = jnp.maximum(zs[g], 0.0) + jnp.log(1.0 + jnp.exp(-jnp.abs(zs[g])))
            if mask is not None:
                sp = jnp.where(mask, sp, 0.0)
            newers.append(jnp.dot(tri_op, sp.astype(BF16), preferred_element_type=F32))
            sps.append(sp)
        pvs = []
        for g in streams:
            wgt = jnp.exp(zs[g] - (sps[g] + newers[g][:tk] + run_ref[g]))
            if mask is not None:
                wgt = jnp.where(mask, wgt, 0.0)
            pvs.append(jnp.dot(vt_ref[0, g * BLK:(g + 1) * BLK, rows], wgt.astype(BF16),
                               preferred_element_type=F32))
        for g in streams:
            acc_ref[g] += pvs[g]
            run_ref[g] += newers[g][tk:tk + 1]

    @pl.when(i < n_ft)
    def _():
        tri_tq = tri(TQ)
        r = lax.broadcasted_iota(jnp.int32, (TQ, w), 0)
        c = lax.broadcasted_iota(jnp.int32, (TQ, w), 1) & (TQ - 1)
        tile(pl.ds(pl.multiple_of(i * TQ, TQ), TQ), tri_tq, r < c)

        def body(jj, carry):
            tile(pl.ds(pl.multiple_of((i - 1 - jj) * TQ, TQ), TQ), tri_tq, None)
            return carry

        lax.fori_loop(0, i, body, 0)

    r = lax.broadcasted_iota(jnp.int32, (N_META, w), 0)
    c = lax.broadcasted_iota(jnp.int32, (N_META, w), 1) & (TQ - 1)
    tile(pl.ds(seq, N_META), tri(N_META), (i < n_ft) | (r < c))

    for g in streams:
        acc = acc_ref[g]
        o_t = jnp.concatenate([acc[:HEAD_DIM, :TQ], acc[HEAD_DIM:, TQ:]], axis=0)
        o_ref[0, :, g * BLK:(g + 1) * BLK] = o_t.T.astype(o_ref.dtype)


def _attn_b(proj, vt, seq):
    bsz, lp, _ = proj.shape
    gw = B_STREAMS * BLK
    pw = B_PART * BLK
    n_part = B_STREAMS // B_PART
    assert (A_Q + 2 * A_KV) % pw == 0 and B_W % gw == 0 and B_STREAMS % B_PART == 0
    qcol = (A_Q + 2 * A_KV) // pw
    kcol = qcol + B_W // pw

    def col_specs(rows, first, row_index):
        return [pl.BlockSpec((1, rows, pw),
                             lambda b, p, i, s=s: (b, row_index(i), first + p * n_part + s))
                for s in range(n_part)]

    return pl.pallas_call(
        functools.partial(_attn_b_kernel, seq=seq),
        grid=(bsz, B_W // gw, pl.cdiv(lp, TQ)),
        in_specs=col_specs(TQ, qcol, lambda i: i) + col_specs(lp, kcol, lambda i: 0) + [
            pl.BlockSpec((1, gw, lp), lambda b, p, i: (b, p, 0)),
        ],
        out_specs=pl.BlockSpec((1, TQ, gw), lambda b, p, i: (b, i, p)),
        out_shape=jax.ShapeDtypeStruct((bsz, lp, B_W), BF16),
        scratch_shapes=[pltpu.VMEM((B_STREAMS, 1, 2 * TQ), F32),
                        pltpu.VMEM((B_STREAMS, BLK, 2 * TQ), F32)],
        compiler_params=_params(("arbitrary", "arbitrary", "arbitrary")),
        name="attn_b",
    )(*([proj] * (2 * n_part)), vt)


def _attn_c_kernel(slopes_ref, lam_ref, g_ref, q_ref, k_ref, vt_ref, o_ref,
                   m_ref, acc_ref, *, seq, lambda_init):
    p = pl.program_id(1)
    i = pl.program_id(2)
    n_ft = seq // TQ
    w = 2 * TQ
    slope = [slopes_ref[p * NSTREAM + g] for g in range(NSTREAM)]
    qd = [_split_heads_rows(q_ref[0, :, g * BLK:(g + 1) * BLK] * SCALE)
          for g in range(NSTREAM)]
    meta = pl.ds(seq, N_META)

    def cols(g):
        return slice(g * BLK, (g + 1) * BLK)

    def vt_ones(g, rows):
        tk = rows.size
        return jnp.concatenate([vt_ref[0, cols(g), rows], jnp.ones((ACC_ROWS - BLK, tk), BF16)],
                               axis=0)

    def meta_scores(g):
        return _nt_dot(k_ref[0, meta, cols(g)], qd[g])

    rm = lax.broadcasted_iota(jnp.int32, (N_META, w), 0)
    cm = lax.broadcasted_iota(jnp.int32, (N_META, w), 1) & (TQ - 1)

    @pl.when(i == n_ft)
    def _():
        dist = jnp.abs(cm - rm).astype(F32)
        for g in range(NSTREAM):
            t = meta_scores(g) - slope[g] * dist
            m0 = jnp.max(t, axis=0, keepdims=True)
            e0 = jnp.exp(t - m0)
            m_ref[g] = m0
            acc_ref[g] = jnp.dot(vt_ones(g, meta), e0.astype(BF16), preferred_element_type=F32)

    @pl.when(i < n_ft)
    def _():
        r = lax.broadcasted_iota(jnp.int32, (TQ, w), 0)
        c = lax.broadcasted_iota(jnp.int32, (TQ, w), 1) & (TQ - 1)
        lane_k = lax.broadcasted_iota(jnp.int32, (TQ, BLK), 1)
        row_k = lax.broadcasted_iota(jnp.int32, (TQ, BLK), 0)
        key_off = jnp.where(lane_k < 3, row_k, 0).astype(F32).astype(BF16)
        lane_q = lax.broadcasted_iota(jnp.int32, (16, BLK), 1)
        qa = []
        for g in range(NSTREAM):
            sl = jnp.full((16, BLK), slope[g], F32)
            s_hi = sl.astype(BF16).astype(F32)
            s_mid = (sl - s_hi).astype(BF16).astype(F32)
            s_lo = (sl - s_hi) - s_mid
            sl3 = jnp.where(lane_q == 0, s_hi,
                            jnp.where(lane_q == 1, s_mid, jnp.where(lane_q == 2, s_lo, 0.0)))
            sl3 = jnp.concatenate([sl3.astype(BF16)] * (w // 16), axis=0)
            qa.append(jnp.concatenate([qd[g], sl3], axis=1))

        def scores(g, rows):
            return _nt_dot(jnp.concatenate([k_ref[0, rows, cols(g)], key_off], axis=1), qa[g])

        diag = pl.ds(pl.multiple_of(i * TQ, TQ), TQ)
        vis = (r >> 6) <= (c >> 6)
        ahead = jnp.where(vis, 2.0 * jnp.maximum(r - c, 0).astype(F32), HIDE)
        meta_off = (rm - (N_META + i * TQ)).astype(F32)
        s_d = [scores(g, diag) for g in range(NSTREAM)]
        s_m = [meta_scores(g) for g in range(NSTREAM)]
        e_d, e_m = [], []
        for g in range(NSTREAM):
            t_d = s_d[g] - slope[g] * ahead
            t_m = s_m[g] + slope[g] * meta_off
            m0 = jnp.maximum(jnp.max(t_d, axis=0, keepdims=True),
                             jnp.max(t_m, axis=0, keepdims=True))
            e_d.append(jnp.exp(t_d - m0).astype(BF16))
            e_m.append(jnp.exp(t_m - m0).astype(BF16))
            m_ref[g] = m0
        for g in range(NSTREAM):
            acc_ref[g] = (jnp.dot(vt_ones(g, diag), e_d[g], preferred_element_type=F32)
                          + jnp.dot(vt_ones(g, meta), e_m[g], preferred_element_type=F32))

        def older_tile(j):
            rows = pl.ds(pl.multiple_of(j * TQ, TQ), TQ)
            s = [scores(g, rows) for g in range(NSTREAM)]
            es, corrs = [], []
            for g in range(NSTREAM):
                cj = slope[g] * ((i - j) * TQ).astype(F32)
                m_old = m_ref[g]
                m_new = jnp.maximum(m_old, jnp.max(s[g], axis=0, keepdims=True) - cj)
                es.append(jnp.exp(s[g] - (m_new + cj)).astype(BF16))
                corrs.append(jnp.exp(m_old - m_new))
                m_ref[g] = m_new
            pvs = [jnp.dot(vt_ones(g, rows), es[g], preferred_element_type=F32)
                   for g in range(NSTREAM)]
            for g in range(NSTREAM):
                acc_ref[g] = corrs[g] * acc_ref[g] + pvs[g]

        def body(j, carry):
            older_tile(j)
            return carry

        lax.fori_loop(0, i, body, 0)

    lv = lam_ref[...]
    t1 = jnp.sum(lv[0:1] * lv[1:2], axis=-1, keepdims=True)
    t2 = jnp.sum(lv[2:3] * lv[3:4], axis=-1, keepdims=True)
    lam = jnp.exp(t1) - jnp.exp(t2) + lambda_init

    for g in range(NSTREAM):
        acc = acc_ref[g]
        o_all = acc[:BLK] / acc[BLK:BLK + 1]
        o_t = o_all[:, :TQ] - lam * o_all[:, TQ:]
        ms = jnp.mean(o_t * o_t, axis=0, keepdims=True)
        o = (o_t * lax.rsqrt(ms + EPS)).T * g_ref[...] * (1.0 - lambda_init)
        o_ref[0, :, cols(g)] = o.astype(o_ref.dtype)


def _attn_c(proj, vt, slopes, lam_vecs, subln_g, lambda_init, seq):
    bsz, lp, _ = proj.shape
    gw = NSTREAM * BLK
    kcol = C_QK // gw
    return pl.pallas_call(
        functools.partial(_attn_c_kernel, seq=seq, lambda_init=lambda_init),
        grid=(bsz, C_QK // gw, pl.cdiv(lp, TQ)),
        in_specs=[
            pl.BlockSpec(memory_space=pltpu.SMEM),
            pl.BlockSpec((4, HEAD_DIM), lambda b, p, i: (0, 0)),
            pl.BlockSpec((1, 2 * HEAD_DIM), lambda b, p, i: (0, 0)),
            pl.BlockSpec((1, TQ, gw), lambda b, p, i: (b, i, p)),
            pl.BlockSpec((1, lp, gw), lambda b, p, i: (b, 0, kcol + p)),
            pl.BlockSpec((1, gw, lp), lambda b, p, i: (b, p, 0)),
        ],
        out_specs=pl.BlockSpec((1, TQ, gw), lambda b, p, i: (b, i, p)),
        out_shape=jax.ShapeDtypeStruct((bsz, lp, C_QK), BF16),
        scratch_shapes=[pltpu.VMEM((NSTREAM, 1, 2 * TQ), F32),
                        pltpu.VMEM((NSTREAM, ACC_ROWS, 2 * TQ), F32)],
        compiler_params=_params(("arbitrary", "arbitrary", "arbitrary")),
        name="attn_c",
    )(slopes, lam_vecs, subln_g.reshape(1, 2 * HEAD_DIM), proj, proj, vt)


def _alibi_slopes(n):
    return jnp.exp2(-8.0 * (jnp.arange(n, dtype=F32) + 1.0) / n)


def kernel(x, meta_tokens, ab_norm, w_in_ab, attn_sinks, w_out_ab, c_norm, w_in_c,
           diff_lambda, diff_subln, w_out_c, mlp_norm, w_mlp_in, w_mlp_out, final_norm):
    bsz, seq, d = x.shape
    lp = seq + BLK
    assert seq % TQ == 0 and d == D_MODEL
    t = bsz * lp
    depth = mlp_norm.shape[0]

    tail = jnp.concatenate([meta_tokens.astype(x.dtype), jnp.zeros((BLK - N_META, d), x.dtype)])
    h = (x, tail)

    tm = lp // 2

    for layer in range(depth):
        li = layer // 2
        if layer % 2 == 0:
            proj = _norm_matmul(h, ab_norm[li], w_in_ab, li, relu2=False, tm=tm, tn=768)
            proj = proj.reshape(bsz, lp, AB_IN)
            vt_a = _transposed_cols(proj, A_Q + A_KV, A_KV)
            out_a = _attn_a(proj, vt_a, _alibi_slopes(A_HEADS), attn_sinks[li].astype(F32), seq)
            vt_b = _transposed_cols(proj, AB_IN - B_W, B_W)
            out_b = _attn_b(proj, vt_b, seq)
            h = _matmul_res([out_a.reshape(t, A_Q), out_b.reshape(t, B_W)], w_out_ab, li, h,
                            tm=lp, tn=512, tk=A_Q + B_W)
        else:
            lambda_init = 0.8 - 0.6 * math.exp(-0.3 * layer)
            proj = _norm_matmul(h, c_norm[li], w_in_c, li, relu2=False, tm=tm, tn=1024)
            proj = proj.reshape(bsz, lp, 3 * C_QK)
            vt_c = _transposed_cols(proj, 2 * C_QK, C_QK)
            out_c = _attn_c(proj, vt_c, _alibi_slopes(C_HEADS), diff_lambda[li].astype(F32),
                            diff_subln[li].astype(F32), lambda_init, seq)
            h = _matmul_res([out_c.reshape(t, C_QK)], w_out_c, li, h, tm=lp, tn=512, tk=2048)
        u = _norm_matmul(h, mlp_norm[layer], w_mlp_in, layer, relu2=True, tm=tm, tn=1024)
        h = _matmul_res([u], w_mlp_out, layer, h, tm=tm, tn=1024, tk=2048)

    return _final_norm(h.reshape(bsz, lp, d), final_norm, seq)
```

```python
import functools
import math

import jax
import jax.numpy as jnp
from jax import lax
from jax.experimental import pallas as pl
from jax.experimental.pallas import tpu as pltpu

D_MODEL = 2048
N_META = 16
CHUNK = 64
HEAD_DIM = 64
EPS = 1e-6
NEG = -1e30
HIDE = 1e32

A_HEADS = 16
A_KV_HEADS = 4
A_GROUP = A_HEADS // A_KV_HEADS
WINDOW_CHUNKS = 2
B_HEADS = 16
C_HEADS = D_MODEL // (2 * HEAD_DIM)
D_FF = 4 * D_MODEL

A_Q = A_HEADS * HEAD_DIM
A_KV = A_KV_HEADS * HEAD_DIM
B_W = B_HEADS * HEAD_DIM
AB_IN = A_Q + 2 * A_KV + 3 * B_W
C_QK = C_HEADS * 2 * HEAD_DIM

BLK = 128
TQ = 256
NSTREAM = 8
B_STREAMS = 8
B_PART = 4
ACC_ROWS = BLK + 16
SCALE = HEAD_DIM ** -0.5

VMEM_LIMIT = 56 * 1024 * 1024

F32 = jnp.float32
BF16 = jnp.bfloat16


def _params(sem, vmem=VMEM_LIMIT):
    return pltpu.CompilerParams(dimension_semantics=sem, vmem_limit_bytes=vmem)


def _nt_dot(a, b):
    return lax.dot_general(a, b, (((1,), (1,)), ((), ())), preferred_element_type=F32)


def _split_heads_rows(x):
    lane = lax.broadcasted_iota(jnp.int32, x.shape, 1)
    zero = jnp.zeros_like(x)
    return jnp.concatenate([jnp.where(lane < HEAD_DIM, x, zero),
                            jnp.where(lane >= HEAD_DIM, x, zero)], axis=0)


def _norm_rows(x, g):
    ms = jnp.mean(x * x, axis=-1, keepdims=True)
    return (x * lax.rsqrt(ms + EPS) * g).astype(BF16)


def _norm_matmul_kernel(*refs, relu2, split):
    x_ref, tail_ref = (refs[0], refs[1]) if split else (refs[0], None)
    g_ref, w_ref, o_ref, hn_ref = refs[-4:]

    @pl.when(pl.program_id(1) == 0)
    def _():
        hn_ref[...] = _norm_rows(x_ref[0] if split else x_ref[...], g_ref[...])
        if split:
            @pl.when(pl.program_id(0) % 2 == 1)
            def _():
                hn_ref[hn_ref.shape[0] - BLK:, :] = _norm_rows(tail_ref[...], g_ref[...])

    acc = jnp.dot(hn_ref[...], w_ref[...].astype(BF16), preferred_element_type=F32)
    if relu2:
        acc = jnp.square(jnp.maximum(acc, 0.0))
    o_ref[...] = acc.astype(o_ref.dtype)


def _norm_matmul(x, g, w_stack, li, *, relu2, tm, tn):
    split = isinstance(x, tuple)
    if split:
        frames, tail = x
        bsz, seq, k = frames.shape
        t = bsz * (seq + BLK)
        assert 2 * tm == seq + BLK
        x_args = [frames, tail]
        x_specs = [pl.BlockSpec((1, tm, k), lambda i, j: (i // 2, i % 2, 0)),
                   pl.BlockSpec((BLK, k), lambda i, j: (0, 0))]
    else:
        t, k = x.shape
        x_args = [x]
        x_specs = [pl.BlockSpec((tm, k), lambda i, j: (i, 0))]
    n = w_stack.shape[2]
    assert t % tm == 0 and n % tn == 0
    return pl.pallas_call(
        functools.partial(_norm_matmul_kernel, relu2=relu2, split=split),
        grid=(t // tm, n // tn),
        in_specs=x_specs + [
            pl.BlockSpec((1, k), lambda i, j: (0, 0)),
            pl.BlockSpec((None, k, tn), lambda i, j: (li, 0, j)),
        ],
        out_specs=pl.BlockSpec((tm, tn), lambda i, j: (i, j)),
        out_shape=jax.ShapeDtypeStruct((t, n), BF16),
        scratch_shapes=[pltpu.VMEM((tm, k), BF16)],
        compiler_params=_params(("arbitrary", "arbitrary")),
        name="norm_matmul_relu2" if relu2 else "norm_matmul",
    )(*x_args, g.reshape(1, k), w_stack)


def _matmul_res_kernel(*refs, n_a, split):
    a_refs, w_ref, r_refs, o_ref = refs[:n_a], refs[n_a], refs[n_a + 1:-1], refs[-1]
    w = w_ref[...].astype(BF16)
    ka = w.shape[0] // n_a
    part = jnp.dot(a_refs[0][...], w[:ka], preferred_element_type=F32)
    for s in range(1, n_a):
        part += jnp.dot(a_refs[s][...], w[s * ka:(s + 1) * ka], preferred_element_type=F32)

    @pl.when(pl.program_id(2) == 0)
    def _():
        if split:
            res = jnp.concatenate([r_refs[0][0], r_refs[1][...]], axis=0)
        else:
            res = r_refs[0][...]
        o_ref[...] = res + part

    @pl.when(pl.program_id(2) != 0)
    def _():
        o_ref[...] += part


def _matmul_res(a_parts, w_stack, li, res, *, tm, tn, tk):
    n_a = len(a_parts)
    t = a_parts[0].shape[0]
    _, k, n = w_stack.shape
    split = isinstance(res, tuple)
    if split:
        assert tm == res[0].shape[1] + BLK
        r_args = list(res)
        r_specs = [pl.BlockSpec((1, tm - BLK, tn), lambda i, j, kk: (i, 0, j)),
                   pl.BlockSpec((BLK, tn), lambda i, j, kk: (0, j))]
    else:
        r_args = [res]
        r_specs = [pl.BlockSpec((tm, tn), lambda i, j, kk: (i, j))]
    assert sum(a.shape[1] for a in a_parts) == k and (n_a == 1 or tk == k)
    assert t % tm == 0 and n % tn == 0 and k % tk == 0
    return pl.pallas_call(
        functools.partial(_matmul_res_kernel, n_a=n_a, split=split),
        grid=(t // tm, n // tn, k // tk),
        in_specs=[pl.BlockSpec((tm, tk // n_a), lambda i, j, kk: (i, kk)) for _ in a_parts] + [
            pl.BlockSpec((None, tk, tn), lambda i, j, kk: (li, kk, j))] + r_specs,
        out_specs=pl.BlockSpec((tm, tn), lambda i, j, kk: (i, j)),
        out_shape=jax.ShapeDtypeStruct((t, n), F32),
        compiler_params=_params(("arbitrary", "arbitrary", "arbitrary")),
        name="matmul_res",
    )(*a_parts, w_stack, *r_args)


def _matmul_res_norm_kernel(a_ref, w_ref, r_ref, g_ref, o_ref, *, n_col):
    j, kk = pl.program_id(1), pl.program_id(2)
    tn = r_ref.shape[1]
    part = jnp.dot(a_ref[...], w_ref[...].astype(BF16), preferred_element_type=F32)
    for c in range(n_col):
        cols = slice(c * tn, (c + 1) * tn)

        @pl.when((j == c) & (kk == 0))
        def _():
            o_ref[0, :, cols] = r_ref[...] + part

        @pl.when((j == c) & (kk != 0))
        def _():
            o_ref[0, :, cols] += part

    @pl.when((j == n_col - 1) & (kk == pl.num_programs(2) - 1))
    def _():
        x = o_ref[0]
        ms = jnp.mean(x * x, axis=-1, keepdims=True)
        o_ref[0] = x * lax.rsqrt(ms + EPS) * g_ref[...]


def _matmul_res_norm(a, w_stack, li, res, g, bsz, seq, *, tn, tk):
    t, k = a.shape
    n = w_stack.shape[2]
    tm = (seq + BLK) // 2
    assert t == 2 * bsz * tm and n % tn == 0 and k % tk == 0
    return pl.pallas_call(
        functools.partial(_matmul_res_norm_kernel, n_col=n // tn),
        grid=(t // tm, n // tn, k // tk),
        in_specs=[
            pl.BlockSpec((tm, tk), lambda i, j, kk: (i, kk)),
            pl.BlockSpec((None, tk, tn), lambda i, j, kk: (li, kk, j)),
            pl.BlockSpec((tm, tn), lambda i, j, kk: (i, j)),
            pl.BlockSpec((1, n), lambda i, j, kk: (0, 0)),
        ],
        out_specs=pl.BlockSpec((1, tm, n), lambda i, j, kk: (i // 2, i % 2, 0)),
        out_shape=jax.ShapeDtypeStruct((bsz, seq, n), F32),
        compiler_params=_params(("arbitrary", "arbitrary", "arbitrary")),
        name="matmul_res_norm",
    )(a, w_stack, res, g.reshape(1, n))


def _transpose_kernel(x_ref, o_ref):
    for c in range(x_ref.shape[2] // BLK):
        o_ref[0, c * BLK:(c + 1) * BLK, :] = x_ref[0, :, c * BLK:(c + 1) * BLK].T


def _transposed_cols(proj, col0, ncols):
    bsz, lp, _ = proj.shape
    cw = math.gcd(col0, ncols, 8 * BLK)
    assert cw % BLK == 0
    return pl.pallas_call(
        _transpose_kernel,
        grid=(bsz, ncols // cw),
        in_specs=[pl.BlockSpec((1, lp, cw), lambda b, c: (b, 0, col0 // cw + c))],
        out_specs=pl.BlockSpec((1, cw, lp), lambda b, c: (b, c, 0)),
        out_shape=jax.ShapeDtypeStruct((bsz, ncols, lp), proj.dtype),
        compiler_params=_params(("arbitrary", "arbitrary")),
        name="transpose_v",
    )(proj)


A_BAND = TQ + BLK
A_VROWS = HEAD_DIM + 16


def _attn_a_kernel(slopes_ref, sinks_ref, q_ref, k_ref, vt_ref, o_ref, *, seq):
    i = pl.program_id(1)
    lp = k_ref.shape[1]
    wl = A_GROUP * TQ
    start = pl.multiple_of(jnp.clip(i * TQ - BLK, 0, lp - A_BAND), BLK)
    band = pl.ds(start, A_BAND)
    meta = pl.ds(seq, N_META)

    def positions(nrows):
        r = lax.broadcasted_iota(jnp.int32, (nrows, TQ), 0)
        u_q = i * TQ + lax.broadcasted_iota(jnp.int32, (nrows, TQ), 1)
        q_frame = u_q < seq
        return r, jnp.where(q_frame, u_q + N_META, u_q - seq), jnp.where(q_frame, 1 + (u_q >> 6), 0)

    r, pos_q, qc = positions(A_BAND)
    u_k = start + r
    kc = 1 + (u_k >> 6)
    vis = (u_k < seq) & (kc <= qc) & (kc >= qc - WINDOW_CHUNKS)
    dist = jnp.where(vis, jnp.abs(pos_q - (u_k + N_META)).astype(F32), HIDE)
    rm, pos_qm, _ = positions(N_META)
    dist_m = jnp.abs(pos_qm - rm).astype(F32)
    head_lane = lax.broadcasted_iota(jnp.int32, (1, wl), 1) >> 8

    def per_head(ref, g):
        v = jnp.zeros((1, wl), F32)
        for a in range(A_GROUP):
            v = jnp.where(head_lane == a, ref[g * A_GROUP + a], v)
        return v

    def dup(kt, g):
        kg = kt[:, g * HEAD_DIM:(g + 1) * HEAD_DIM]
        return jnp.concatenate([kg, kg], axis=1)

    groups = range(A_KV_HEADS)
    k_band, k_meta = k_ref[0, band, :], k_ref[0, meta, :]
    qd = [jnp.concatenate(
        [_split_heads_rows(q_ref[0, :, (2 * g + p) * BLK:(2 * g + p + 1) * BLK] * SCALE)
         for p in range(2)], axis=0) for g in groups]
    s_b = [_nt_dot(dup(k_band, g), qd[g]) for g in groups]
    s_m = [_nt_dot(dup(k_meta, g), qd[g]) for g in groups]
    e_b, e_m, extra = [], [], []
    for g in groups:
        slope, sink = per_head(slopes_ref, g), per_head(sinks_ref, g)
        t_b = s_b[g] - slope * jnp.concatenate([dist] * A_GROUP, axis=1)
        t_m = s_m[g] - slope * jnp.concatenate([dist_m] * A_GROUP, axis=1)
        mx = jnp.maximum(jnp.maximum(jnp.max(t_b, axis=0, keepdims=True),
                                     jnp.max(t_m, axis=0, keepdims=True)), sink)
        e_b.append(jnp.exp(t_b - mx).astype(BF16))
        e_m.append(jnp.exp(t_m - mx).astype(BF16))
        extra.append(jnp.exp(sink - mx))

    def vt_ones(g, rows):
        return jnp.concatenate([vt_ref[0, g * HEAD_DIM:(g + 1) * HEAD_DIM, rows],
                                jnp.ones((A_VROWS - HEAD_DIM, rows.size), BF16)], axis=0)

    acc = [jnp.dot(vt_ones(g, band), e_b[g], preferred_element_type=F32)
           + jnp.dot(vt_ones(g, meta), e_m[g], preferred_element_type=F32) for g in groups]
    for g in groups:
        o_t = acc[g][:HEAD_DIM] / (acc[g][HEAD_DIM:HEAD_DIM + 1] + extra[g])
        for p in range(2):
            pair = jnp.concatenate([o_t[:, (2 * p) * TQ:(2 * p + 1) * TQ],
                                    o_t[:, (2 * p + 1) * TQ:(2 * p + 2) * TQ]], axis=0)
            o_ref[0, :, (2 * g + p) * BLK:(2 * g + p + 1) * BLK] = pair.T.astype(o_ref.dtype)


def _attn_a(proj, vt, slopes, sinks, seq):
    bsz, lp, _ = proj.shape
    kcol = A_Q // A_KV
    return pl.pallas_call(
        functools.partial(_attn_a_kernel, seq=seq),
        grid=(bsz, pl.cdiv(lp, TQ)),
        in_specs=[
            pl.BlockSpec(memory_space=pltpu.SMEM),
            pl.BlockSpec(memory_space=pltpu.SMEM),
            pl.BlockSpec((1, TQ, A_Q), lambda b, i: (b, i, 0)),
            pl.BlockSpec((1, lp, A_KV), lambda b, i: (b, 0, kcol)),
            pl.BlockSpec((1, A_KV, lp), lambda b, i: (b, 0, 0)),
        ],
        out_specs=pl.BlockSpec((1, TQ, A_Q), lambda b, i: (b, i, 0)),
        out_shape=jax.ShapeDtypeStruct((bsz, lp, A_Q), BF16),
        compiler_params=_params(("arbitrary", "arbitrary")),
        name="attn_a",
    )(slopes, sinks, proj, proj, vt)


def _attn_b_kernel(*refs, seq):
    n_part = B_STREAMS // B_PART
    q_refs, k_refs = refs[:n_part], refs[n_part:2 * n_part]
    vt_ref, o_ref, run_ref, acc_ref = refs[2 * n_part:]
    i = pl.program_id(2)
    n_ft = seq // TQ
    w = 2 * TQ

    def part(part_refs, g, rows):
        lo = (g % B_PART) * BLK
        return part_refs[g // B_PART][0, rows, lo:lo + BLK]

    streams = range(B_STREAMS)
    qd = [_split_heads_rows(part(q_refs, g, slice(None)) * SCALE)
          for g in streams]
    run_ref[...] = jnp.zeros_like(run_ref)
    acc_ref[...] = jnp.zeros_like(acc_ref)

    def tri(tk):
        s_idx = lax.broadcasted_iota(jnp.int32, (tk + 16, tk), 0)
        j_idx = lax.broadcasted_iota(jnp.int32, (tk + 16, tk), 1)
        return jnp.where((j_idx > s_idx) | (s_idx >= tk), 1.0, 0.0).astype(BF16)

    def tile(rows, tri_op, mask):
        zs = [_nt_dot(part(k_refs, g, rows), qd[g]) for g in streams]
        tk = rows.size
        sps, newers = [], []
        for g in streams:
            sp = jnp.maximum(zs[g], 0.0) + jnp.log(1.0 + jnp.exp(-jnp.abs(zs[g])))
            if mask is not None:
                sp = jnp.where(mask, sp, 0.0)
            newers.append(jnp.dot(tri_op, sp.astype(BF16), preferred_element_type=F32))
            sps.append(sp)
        pvs = []
        for g in streams:
            wgt = jnp.exp(zs[g] - (sps[g] + newers[g][:tk] + run_ref[g]))
            if mask is not None:
                wgt = jnp.where(mask, wgt, 0.0)
            pvs.append(jnp.dot(vt_ref[0, g * BLK:(g + 1) * BLK, rows], wgt.astype(BF16),
                               preferred_element_type=F32))
        for g in streams:
            acc_ref[g] += pvs[g]
            run_ref[g] += newers[g][tk:tk + 1]

    @pl.when(i < n_ft)
    def _():
        tri_tq = tri(TQ)
        r = lax.broadcasted_iota(jnp.int32, (TQ, w), 0)
        c = lax.broadcasted_iota(jnp.int32, (TQ, w), 1) & (TQ - 1)
        tile(pl.ds(pl.multiple_of(i * TQ, TQ), TQ), tri_tq, r < c)

        def body(jj, carry):
            tile(pl.ds(pl.multiple_of((i - 1 - jj) * TQ, TQ), TQ), tri_tq, None)
            return carry

        lax.fori_loop(0, i, body, 0)

    r = lax.broadcasted_iota(jnp.int32, (N_META, w), 0)
    c = lax.broadcasted_iota(jnp.int32, (N_META, w), 1) & (TQ - 1)
    tile(pl.ds(seq, N_META), tri(N_META), (i < n_ft) | (r < c))

    for g in streams:
        acc = acc_ref[g]
        o_t = jnp.concatenate([acc[:HEAD_DIM, :TQ], acc[HEAD_DIM:, TQ:]], axis=0)
        o_ref[0, :, g * BLK:(g + 1) * BLK] = o_t.T.astype(o_ref.dtype)


def _attn_b(proj, vt, seq):
    bsz, lp, _ = proj.shape
    gw = B_STREAMS * BLK
    pw = B_PART * BLK
    n_part = B_STREAMS // B_PART
    assert (A_Q + 2 * A_KV) % pw == 0 and B_W % gw == 0 and B_STREAMS % B_PART == 0
    qcol = (A_Q + 2 * A_KV) // pw
    kcol = qcol + B_W // pw

    def col_specs(rows, first, row_index):
        return [pl.BlockSpec((1, rows, pw),
                             lambda b, p, i, s=s: (b, row_index(i), first + p * n_part + s))
                for s in range(n_part)]

    return pl.pallas_call(
        functools.partial(_attn_b_kernel, seq=seq),
        grid=(bsz, B_W // gw, pl.cdiv(lp, TQ)),
        in_specs=col_specs(TQ, qcol, lambda i: i) + col_specs(lp, kcol, lambda i: 0) + [
            pl.BlockSpec((1, gw, lp), lambda b, p, i: (b, p, 0)),
        ],
        out_specs=pl.BlockSpec((1, TQ, gw), lambda b, p, i: (b, i, p)),
        out_shape=jax.ShapeDtypeStruct((bsz, lp, B_W), BF16),
        scratch_shapes=[pltpu.VMEM((B_STREAMS, 1, 2 * TQ), F32),
                        pltpu.VMEM((B_STREAMS, BLK, 2 * TQ), F32)],
        compiler_params=_params(("arbitrary", "arbitrary", "arbitrary")),
        name="attn_b",
    )(*([proj] * (2 * n_part)), vt)


def _attn_c_kernel(slopes_ref, lam_ref, g_ref, q_ref, k_ref, vt_ref, o_ref,
                   m_ref, acc_ref, *, seq, lambda_init):
    p = pl.program_id(1)
    i = pl.program_id(2)
    n_ft = seq // TQ
    w = 2 * TQ
    slope = [slopes_ref[p * NSTREAM + g] for g in range(NSTREAM)]
    qd = [_split_heads_rows(q_ref[0, :, g * BLK:(g + 1) * BLK] * SCALE)
          for g in range(NSTREAM)]
    meta = pl.ds(seq, N_META)

    def cols(g):
        return slice(g * BLK, (g + 1) * BLK)

    def vt_ones(g, rows):
        tk = rows.size
        return jnp.concatenate([vt_ref[0, cols(g), rows], jnp.ones((ACC_ROWS - BLK, tk), BF16)],
                               axis=0)

    def meta_scores(g):
        return _nt_dot(k_ref[0, meta, cols(g)], qd[g])

    rm = lax.broadcasted_iota(jnp.int32, (N_META, w), 0)
    cm = lax.broadcasted_iota(jnp.int32, (N_META, w), 1) & (TQ - 1)

    @pl.when(i == n_ft)
    def _():
        dist = jnp.abs(cm - rm).astype(F32)
        for g in range(NSTREAM):
            t = meta_scores(g) - slope[g] * dist
            m0 = jnp.max(t, axis=0, keepdims=True)
            e0 = jnp.exp(t - m0)
            m_ref[g] = m0
            acc_ref[g] = jnp.dot(vt_ones(g, meta), e0.astype(BF16), preferred_element_type=F32)

    @pl.when(i < n_ft)
    def _():
        r = lax.broadcasted_iota(jnp.int32, (TQ, w), 0)
        c = lax.broadcasted_iota(jnp.int32, (TQ, w), 1) & (TQ - 1)
        lane_k = lax.broadcasted_iota(jnp.int32, (TQ, BLK), 1)
        row_k = lax.broadcasted_iota(jnp.int32, (TQ, BLK), 0)
        key_off = jnp.where(lane_k < 3, row_k, 0).astype(F32).astype(BF16)
        lane_q = lax.broadcasted_iota(jnp.int32, (16, BLK), 1)
        qa = []
        for g in range(NSTREAM):
            sl = jnp.full((16, BLK), slope[g], F32)
            s_hi = sl.astype(BF16).astype(F32)
            s_mid = (sl - s_hi).astype(BF16).astype(F32)
            s_lo = (sl - s_hi) - s_mid
            sl3 = jnp.where(lane_q == 0, s_hi,
                            jnp.where(lane_q == 1, s_mid, jnp.where(lane_q == 2, s_lo, 0.0)))
            sl3 = jnp.concatenate([sl3.astype(BF16)] * (w // 16), axis=0)
            qa.append(jnp.concatenate([qd[g], sl3], axis=1))

        def scores(g, rows):
            return _nt_dot(jnp.concatenate([k_ref[0, rows, cols(g)], key_off], axis=1), qa[g])

        diag = pl.ds(pl.multiple_of(i * TQ, TQ), TQ)
        vis = (r >> 6) <= (c >> 6)
        ahead = jnp.where(vis, 2.0 * jnp.maximum(r - c, 0).astype(F32), HIDE)
        meta_off = (rm - (N_META + i * TQ)).astype(F32)
        s_d = [scores(g, diag) for g in range(NSTREAM)]
        s_m = [meta_scores(g) for g in range(NSTREAM)]
        e_d, e_m = [], []
        for g in range(NSTREAM):
            t_d = s_d[g] - slope[g] * ahead
            t_m = s_m[g] + slope[g] * meta_off
            m0 = jnp.maximum(jnp.max(t_d, axis=0, keepdims=True),
                             jnp.max(t_m, axis=0, keepdims=True))
            e_d.append(jnp.exp(t_d - m0).astype(BF16))
            e_m.append(jnp.exp(t_m - m0).astype(BF16))
            m_ref[g] = m0
        for g in range(NSTREAM):
            acc_ref[g] = (jnp.dot(vt_ones(g, diag), e_d[g], preferred_element_type=F32)
                          + jnp.dot(vt_ones(g, meta), e_m[g], preferred_element_type=F32))

        def older_tile(j):
            rows = pl.ds(pl.multiple_of(j * TQ, TQ), TQ)
            s = [scores(g, rows) for g in range(NSTREAM)]
            es, corrs = [], []
            for g in range(NSTREAM):
                cj = slope[g] * ((i - j) * TQ).astype(F32)
                m_old = m_ref[g]
                m_new = jnp.maximum(m_old, jnp.max(s[g], axis=0, keepdims=True) - cj)
                es.append(jnp.exp(s[g] - (m_new + cj)).astype(BF16))
                corrs.append(jnp.exp(m_old - m_new))
                m_ref[g] = m_new
            pvs = [jnp.dot(vt_ones(g, rows), es[g], preferred_element_type=F32)
                   for g in range(NSTREAM)]
            for g in range(NSTREAM):
                acc_ref[g] = corrs[g] * acc_ref[g] + pvs[g]

        def body(j, carry):
            older_tile(j)
            return carry

        lax.fori_loop(0, i, body, 0)

    lv = lam_ref[...]
    t1 = jnp.sum(lv[0:1] * lv[1:2], axis=-1, keepdims=True)
    t2 = jnp.sum(lv[2:3] * lv[3:4], axis=-1, keepdims=True)
    lam = jnp.exp(t1) - jnp.exp(t2) + lambda_init

    for g in range(NSTREAM):
        acc = acc_ref[g]
        o_all = acc[:BLK] / acc[BLK:BLK + 1]
        o_t = o_all[:, :TQ] - lam * o_all[:, TQ:]
        ms = jnp.mean(o_t * o_t, axis=0, keepdims=True)
        o = (o_t * lax.rsqrt(ms + EPS)).T * g_ref[...] * (1.0 - lambda_init)
        o_ref[0, :, cols(g)] = o.astype(o_ref.dtype)


def _attn_c(proj, vt, slopes, lam_vecs, subln_g, lambda_init, seq):
    bsz, lp, _ = proj.shape
    gw = NSTREAM * BLK
    kcol = C_QK // gw
    return pl.pallas_call(
        functools.partial(_attn_c_kernel, seq=seq, lambda_init=lambda_init),
        grid=(bsz, C_QK // gw, pl.cdiv(lp, TQ)),
        in_specs=[
            pl.BlockSpec(memory_space=pltpu.SMEM),
            pl.BlockSpec((4, HEAD_DIM), lambda b, p, i: (0, 0)),
            pl.BlockSpec((1, 2 * HEAD_DIM), lambda b, p, i: (0, 0)),
            pl.BlockSpec((1, TQ, gw), lambda b, p, i: (b, i, p)),
            pl.BlockSpec((1, lp, gw), lambda b, p, i: (b, 0, kcol + p)),
            pl.BlockSpec((1, gw, lp), lambda b, p, i: (b, p, 0)),
        ],
        out_specs=pl.BlockSpec((1, TQ, gw), lambda b, p, i: (b, i, p)),
        out_shape=jax.ShapeDtypeStruct((bsz, lp, C_QK), BF16),
        scratch_shapes=[pltpu.VMEM((NSTREAM, 1, 2 * TQ), F32),
                        pltpu.VMEM((NSTREAM, ACC_ROWS, 2 * TQ), F32)],
        compiler_params=_params(("arbitrary", "arbitrary", "arbitrary")),
        name="attn_c",
    )(slopes, lam_vecs, subln_g.reshape(1, 2 * HEAD_DIM), proj, proj, vt)


def _alibi_slopes(n):
    return jnp.exp2(-8.0 * (jnp.arange(n, dtype=F32) + 1.0) / n)


def kernel(x, meta_tokens, ab_norm, w_in_ab, attn_sinks, w_out_ab, c_norm, w_in_c,
           diff_lambda, diff_subln, w_out_c, mlp_norm, w_mlp_in, w_mlp_out, final_norm):
    bsz, seq, d = x.shape
    lp = seq + BLK
    assert seq % TQ == 0 and d == D_MODEL
    t = bsz * lp
    depth = mlp_norm.shape[0]

    tail = jnp.concatenate([meta_tokens.astype(x.dtype), jnp.zeros((BLK - N_META, d), x.dtype)])
    h = (x, tail)

    tm = lp // 2

    for layer in range(depth):
        li = layer // 2
        if layer % 2 == 0:
            proj = _norm_matmul(h, ab_norm[li], w_in_ab, li, relu2=False, tm=tm, tn=768)
            proj = proj.reshape(bsz, lp, AB_IN)
            vt_a = _transposed_cols(proj, A_Q + A_KV, A_KV)
            out_a = _attn_a(proj, vt_a, _alibi_slopes(A_HEADS), attn_sinks[li].astype(F32), seq)
            vt_b = _transposed_cols(proj, AB_IN - B_W, B_W)
            out_b = _attn_b(proj, vt_b, seq)
            h = _matmul_res([out_a.reshape(t, A_Q), out_b.reshape(t, B_W)], w_out_ab, li, h,
                            tm=lp, tn=512, tk=A_Q + B_W)
        else:
            lambda_init = 0.8 - 0.6 * math.exp(-0.3 * layer)
            proj = _norm_matmul(h, c_norm[li], w_in_c, li, relu2=False, tm=tm, tn=1024)
            proj = proj.reshape(bsz, lp, 3 * C_QK)
            vt_c = _transposed_cols(proj, 2 * C_QK, C_QK)
            out_c = _attn_c(proj, vt_c, _alibi_slopes(C_HEADS), diff_lambda[li].astype(F32),
                            diff_subln[li].astype(F32), lambda_init, seq)
            h = _matmul_res([out_c.reshape(t, C_QK)], w_out_c, li, h, tm=lp, tn=512, tk=2048)
        u = _norm_matmul(h, mlp_norm[layer], w_mlp_in, layer, relu2=True, tm=tm, tn=1024)
        if layer == depth - 1:
            return _matmul_res_norm(u, w_mlp_out, layer, h, final_norm, bsz, seq, tn=512, tk=2048)
        h = _matmul_res([u], w_mlp_out, layer, h, tm=tm, tn=1024, tk=2048)
```

```python
import functools
import math

import jax
import jax.numpy as jnp
from jax import lax
from jax.experimental import pallas as pl
from jax.experimental.pallas import tpu as pltpu

D_MODEL = 2048
N_META = 16
CHUNK = 64
CHUNK_SHIFT = CHUNK.bit_length() - 1
HEAD_DIM = 64
EPS = 1e-6
HIDE = 1e32

A_HEADS = 16
A_KV_HEADS = 4
A_GROUP = A_HEADS // A_KV_HEADS
WINDOW_CHUNKS = 2
B_HEADS = 16
C_HEADS = D_MODEL // (2 * HEAD_DIM)

A_Q = A_HEADS * HEAD_DIM
A_KV = A_KV_HEADS * HEAD_DIM
B_W = B_HEADS * HEAD_DIM
AB_IN = A_Q + 2 * A_KV + 3 * B_W
C_QK = C_HEADS * 2 * HEAD_DIM

BLK = 128
TQ = 256
NSTREAM = 8
B_STREAMS = 8
B_PART = 4
ACC_ROWS = BLK + 16
SCALE = HEAD_DIM ** -0.5

V7X_VMEM_BYTES = 64 * 1024 * 1024
VMEM_LIMIT = V7X_VMEM_BYTES * 7 // 8

F32 = jnp.float32
BF16 = jnp.bfloat16


def _params(sem, vmem=VMEM_LIMIT):
    return pltpu.CompilerParams(dimension_semantics=sem, vmem_limit_bytes=vmem)


def _nt_dot(a, b):
    return lax.dot_general(a, b, (((1,), (1,)), ((), ())), preferred_element_type=F32)


def _split_heads_rows(x):
    lane = lax.broadcasted_iota(jnp.int32, x.shape, 1)
    zero = jnp.zeros_like(x)
    return jnp.concatenate([jnp.where(lane < HEAD_DIM, x, zero),
                            jnp.where(lane >= HEAD_DIM, x, zero)], axis=0)


def _norm_rows(x, g):
    ms = jnp.mean(x * x, axis=-1, keepdims=True)
    return (x * lax.rsqrt(ms + EPS) * g).astype(BF16)


def _norm_matmul_kernel(*refs, relu2, split):
    x_ref, tail_ref = (refs[0], refs[1]) if split else (refs[0], None)
    g_ref, w_ref, o_ref, hn_ref = refs[-4:]

    @pl.when(pl.program_id(1) == 0)
    def _():
        hn_ref[...] = _norm_rows(x_ref[0] if split else x_ref[...], g_ref[...])
        if split:
            @pl.when(pl.program_id(0) % 2 == 1)
            def _():
                hn_ref[hn_ref.shape[0] - BLK:, :] = _norm_rows(tail_ref[...], g_ref[...])

    acc = jnp.dot(hn_ref[...], w_ref[...].astype(BF16), preferred_element_type=F32)
    if relu2:
        acc = jnp.square(jnp.maximum(acc, 0.0))
    o_ref[...] = acc.astype(o_ref.dtype)


def _norm_matmul(x, g, w_stack, li, *, relu2, tm, tn):
    split = isinstance(x, tuple)
    if split:
        frames, tail = x
        bsz, seq, k = frames.shape
        t = bsz * (seq + BLK)
        assert 2 * tm == seq + BLK
        x_args = [frames, tail]
        x_specs = [pl.BlockSpec((1, tm, k), lambda i, j: (i // 2, i % 2, 0)),
                   pl.BlockSpec((BLK, k), lambda i, j: (0, 0))]
    else:
        t, k = x.shape
        x_args = [x]
        x_specs = [pl.BlockSpec((tm, k), lambda i, j: (i, 0))]
    n = w_stack.shape[2]
    assert t % tm == 0 and n % tn == 0
    return pl.pallas_call(
        functools.partial(_norm_matmul_kernel, relu2=relu2, split=split),
        grid=(t // tm, n // tn),
        in_specs=x_specs + [
            pl.BlockSpec((1, k), lambda i, j: (0, 0)),
            pl.BlockSpec((None, k, tn), lambda i, j: (li, 0, j)),
        ],
        out_specs=pl.BlockSpec((tm, tn), lambda i, j: (i, j)),
        out_shape=jax.ShapeDtypeStruct((t, n), BF16),
        scratch_shapes=[pltpu.VMEM((tm, k), BF16)],
        compiler_params=_params(("arbitrary", "arbitrary")),
        name="norm_matmul_relu2" if relu2 else "norm_matmul",
    )(*x_args, g.reshape(1, k), w_stack)


def _matmul_res_kernel(*refs, n_a, split):
    a_refs, w_ref, r_refs, o_ref = refs[:n_a], refs[n_a], refs[n_a + 1:-1], refs[-1]
    w = w_ref[...].astype(BF16)
    ka = w.shape[0] // n_a
    part = jnp.dot(a_refs[0][...], w[:ka], preferred_element_type=F32)
    for s in range(1, n_a):
        part += jnp.dot(a_refs[s][...], w[s * ka:(s + 1) * ka], preferred_element_type=F32)

    @pl.when(pl.program_id(2) == 0)
    def _():
        if split:
            res = jnp.concatenate([r_refs[0][0], r_refs[1][...]], axis=0)
        else:
            res = r_refs[0][...]
        o_ref[...] = res + part

    @pl.when(pl.program_id(2) != 0)
    def _():
        o_ref[...] += part


def _matmul_res(a_parts, w_stack, li, res, *, tm, tn, tk):
    n_a = len(a_parts)
    t = a_parts[0].shape[0]
    _, k, n = w_stack.shape
    split = isinstance(res, tuple)
    if split:
        assert tm == res[0].shape[1] + BLK
        r_args = list(res)
        r_specs = [pl.BlockSpec((1, tm - BLK, tn), lambda i, j, kk: (i, 0, j)),
                   pl.BlockSpec((BLK, tn), lambda i, j, kk: (0, j))]
    else:
        r_args = [res]
        r_specs = [pl.BlockSpec((tm, tn), lambda i, j, kk: (i, j))]
    assert sum(a.shape[1] for a in a_parts) == k and (n_a == 1 or tk == k)
    assert t % tm == 0 and n % tn == 0 and k % tk == 0
    return pl.pallas_call(
        functools.partial(_matmul_res_kernel, n_a=n_a, split=split),
        grid=(t // tm, n // tn, k // tk),
        in_specs=[pl.BlockSpec((tm, tk // n_a), lambda i, j, kk: (i, kk)) for _ in a_parts] + [
            pl.BlockSpec((None, tk, tn), lambda i, j, kk: (li, kk, j))] + r_specs,
        out_specs=pl.BlockSpec((tm, tn), lambda i, j, kk: (i, j)),
        out_shape=jax.ShapeDtypeStruct((t, n), F32),
        compiler_params=_params(("arbitrary", "arbitrary", "arbitrary")),
        name="matmul_res",
    )(*a_parts, w_stack, *r_args)


def _final_norm_kernel(x_ref, g_ref, o_ref):
    x = x_ref[0]
    ms = jnp.mean(x * x, axis=-1, keepdims=True)
    o_ref[0] = x * lax.rsqrt(ms + EPS) * g_ref[...]


def _final_norm(h3, g, seq):
    bsz, lp, d = h3.shape
    rows = 4 * BLK
    assert seq % rows == 0
    return pl.pallas_call(
        _final_norm_kernel,
        grid=(bsz, seq // rows),
        in_specs=[
            pl.BlockSpec((1, rows, d), lambda b, r: (b, r, 0)),
            pl.BlockSpec((1, d), lambda b, r: (0, 0)),
        ],
        out_specs=pl.BlockSpec((1, rows, d), lambda b, r: (b, r, 0)),
        out_shape=jax.ShapeDtypeStruct((bsz, seq, d), F32),
        compiler_params=_params(("arbitrary", "arbitrary")),
        name="final_norm",
    )(h3, g.reshape(1, d))


def _transpose_kernel(x_ref, o_ref):
    for c in range(x_ref.shape[2] // BLK):
        o_ref[0, c * BLK:(c + 1) * BLK, :] = x_ref[0, :, c * BLK:(c + 1) * BLK].T


def _transposed_cols(proj, col0, ncols):
    bsz, lp, _ = proj.shape
    cw = math.gcd(col0, ncols, 8 * BLK)
    assert cw % BLK == 0
    return pl.pallas_call(
        _transpose_kernel,
        grid=(bsz, ncols // cw),
        in_specs=[pl.BlockSpec((1, lp, cw), lambda b, c: (b, 0, col0 // cw + c))],
        out_specs=pl.BlockSpec((1, cw, lp), lambda b, c: (b, c, 0)),
        out_shape=jax.ShapeDtypeStruct((bsz, ncols, lp), proj.dtype),
        compiler_params=_params(("arbitrary", "arbitrary")),
        name="transpose_v",
    )(proj)


A_BAND = TQ + BLK
A_VROWS = HEAD_DIM + 16


def _attn_a_kernel(slopes_ref, sinks_ref, q_ref, k_ref, vt_ref, o_ref, *, seq):
    i = pl.program_id(1)
    lp = k_ref.shape[1]
    wl = A_GROUP * TQ
    start = pl.multiple_of(jnp.clip(i * TQ - BLK, 0, lp - A_BAND), BLK)
    band = pl.ds(start, A_BAND)
    meta = pl.ds(seq, N_META)

    def positions(nrows):
        r = lax.broadcasted_iota(jnp.int32, (nrows, TQ), 0)
        u_q = i * TQ + lax.broadcasted_iota(jnp.int32, (nrows, TQ), 1)
        q_frame = u_q < seq
        return (r, jnp.where(q_frame, u_q + N_META, u_q - seq),
                jnp.where(q_frame, 1 + (u_q >> CHUNK_SHIFT), 0))

    r, pos_q, qc = positions(A_BAND)
    u_k = start + r
    kc = 1 + (u_k >> CHUNK_SHIFT)
    vis = (u_k < seq) & (kc <= qc) & (kc >= qc - WINDOW_CHUNKS)
    dist = jnp.where(vis, jnp.abs(pos_q - (u_k + N_META)).astype(F32), HIDE)
    rm, pos_qm, _ = positions(N_META)
    dist_m = jnp.abs(pos_qm - rm).astype(F32)
    head_lane = lax.broadcasted_iota(jnp.int32, (1, wl), 1) >> (TQ.bit_length() - 1)

    def per_head(ref, g):
        v = jnp.zeros((1, wl), F32)
        for a in range(A_GROUP):
            v = jnp.where(head_lane == a, ref[g * A_GROUP + a], v)
        return v

    def dup(kt, g):
        kg = kt[:, g * HEAD_DIM:(g + 1) * HEAD_DIM]
        return jnp.concatenate([kg, kg], axis=1)

    groups = range(A_KV_HEADS)
    k_band, k_meta = k_ref[0, band, :], k_ref[0, meta, :]
    qd = [jnp.concatenate(
        [_split_heads_rows(q_ref[0, :, (2 * g + p) * BLK:(2 * g + p + 1) * BLK] * SCALE)
         for p in range(2)], axis=0) for g in groups]
    s_b = [_nt_dot(dup(k_band, g), qd[g]) for g in groups]
    s_m = [_nt_dot(dup(k_meta, g), qd[g]) for g in groups]
    e_b, e_m, extra = [], [], []
    for g in groups:
        slope, sink = per_head(slopes_ref, g), per_head(sinks_ref, g)
        t_b = s_b[g] - slope * jnp.concatenate([dist] * A_GROUP, axis=1)
        t_m = s_m[g] - slope * jnp.concatenate([dist_m] * A_GROUP, axis=1)
        mx = jnp.maximum(jnp.maximum(jnp.max(t_b, axis=0, keepdims=True),
                                     jnp.max(t_m, axis=0, keepdims=True)), sink)
        e_b.append(jnp.exp(t_b - mx).astype(BF16))
        e_m.append(jnp.exp(t_m - mx).astype(BF16))
        extra.append(jnp.exp(sink - mx))

    def vt_ones(g, rows):
        return jnp.concatenate([vt_ref[0, g * HEAD_DIM:(g + 1) * HEAD_DIM, rows],
                                jnp.ones((A_VROWS - HEAD_DIM, rows.size), BF16)], axis=0)

    acc = [jnp.dot(vt_ones(g, band), e_b[g], preferred_element_type=F32)
           + jnp.dot(vt_ones(g, meta), e_m[g], preferred_element_type=F32) for g in groups]
    for g in groups:
        o_t = acc[g][:HEAD_DIM] / (acc[g][HEAD_DIM:HEAD_DIM + 1] + extra[g])
        for p in range(2):
            pair = jnp.concatenate([o_t[:, (2 * p) * TQ:(2 * p + 1) * TQ],
                                    o_t[:, (2 * p + 1) * TQ:(2 * p + 2) * TQ]], axis=0)
            o_ref[0, :, (2 * g + p) * BLK:(2 * g + p + 1) * BLK] = pair.T.astype(o_ref.dtype)


def _attn_a(proj, vt, slopes, sinks, seq):
    bsz, lp, _ = proj.shape
    kcol = A_Q // A_KV
    return pl.pallas_call(
        functools.partial(_attn_a_kernel, seq=seq),
        grid=(bsz, pl.cdiv(lp, TQ)),
        in_specs=[
            pl.BlockSpec(memory_space=pltpu.SMEM),
            pl.BlockSpec(memory_space=pltpu.SMEM),
            pl.BlockSpec((1, TQ, A_Q), lambda b, i: (b, i, 0)),
            pl.BlockSpec((1, lp, A_KV), lambda b, i: (b, 0, kcol)),
            pl.BlockSpec((1, A_KV, lp), lambda b, i: (b, 0, 0)),
        ],
        out_specs=pl.BlockSpec((1, TQ, A_Q), lambda b, i: (b, i, 0)),
        out_shape=jax.ShapeDtypeStruct((bsz, lp, A_Q), BF16),
        compiler_params=_params(("arbitrary", "arbitrary")),
        name="attn_a",
    )(slopes, sinks, proj, proj, vt)


def _attn_b_kernel(*refs, seq):
    n_part = B_STREAMS // B_PART
    q_refs, k_refs = refs[:n_part], refs[n_part:2 * n_part]
    vt_ref, o_ref, run_ref, acc_ref = refs[2 * n_part:]
    i = pl.program_id(2)
    n_ft = seq // TQ
    w = 2 * TQ

    def part(part_refs, g, rows):
        lo = (g % B_PART) * BLK
        return part_refs[g // B_PART][0, rows, lo:lo + BLK]

    streams = range(B_STREAMS)
    qd = [_split_heads_rows(part(q_refs, g, slice(None)) * SCALE)
          for g in streams]
    run_ref[...] = jnp.zeros_like(run_ref)
    acc_ref[...] = jnp.zeros_like(acc_ref)

    def tri(tk):
        s_idx = lax.broadcasted_iota(jnp.int32, (tk + 16, tk), 0)
        j_idx = lax.broadcasted_iota(jnp.int32, (tk + 16, tk), 1)
        return jnp.where((j_idx > s_idx) | (s_idx >= tk), 1.0, 0.0).astype(BF16)

    def tile(rows, tri_op, mask):
        zs = [_nt_dot(part(k_refs, g, rows), qd[g]) for g in streams]
        tk = rows.size
        sps, newers = [], []
        for g in streams:
            sp = jnp.maximum(zs[g], 0.0) + jnp.log(1.0 + jnp.exp(-jnp.abs(zs[g])))
            if mask is not None:
                sp = jnp.where(mask, sp, 0.0)
            newers.append(jnp.dot(tri_op, sp.astype(BF16), preferred_element_type=F32))
            sps.append(sp)
        pvs = []
        for g in streams:
            wgt = jnp.exp(zs[g] - (sps[g] + newers[g][:tk] + run_ref[g]))
            if mask is not None:
                wgt = jnp.where(mask, wgt, 0.0)
            pvs.append(jnp.dot(vt_ref[0, g * BLK:(g + 1) * BLK, rows], wgt.astype(BF16),
                               preferred_element_type=F32))
        for g in streams:
            acc_ref[g] += pvs[g]
            run_ref[g] += newers[g][tk:tk + 1]

    @pl.when(i < n_ft)
    def _():
        tri_tq = tri(TQ)
        r = lax.broadcasted_iota(jnp.int32, (TQ, w), 0)
        c = lax.broadcasted_iota(jnp.int32, (TQ, w), 1) & (TQ - 1)
        tile(pl.ds(pl.multiple_of(i * TQ, TQ), TQ), tri_tq, r < c)

        def body(jj, carry):
            tile(pl.ds(pl.multiple_of((i - 1 - jj) * TQ, TQ), TQ), tri_tq, None)
            return carry

        lax.fori_loop(0, i, body, 0)

    r = lax.broadcasted_iota(jnp.int32, (N_META, w), 0)
    c = lax.broadcasted_iota(jnp.int32, (N_META, w), 1) & (TQ - 1)
    tile(pl.ds(seq, N_META), tri(N_META), (i < n_ft) | (r < c))

    for g in streams:
        acc = acc_ref[g]
        o_t = jnp.concatenate([acc[:HEAD_DIM, :TQ], acc[HEAD_DIM:, TQ:]], axis=0)
        o_ref[0, :, g * BLK:(g + 1) * BLK] = o_t.T.astype(o_ref.dtype)


def _attn_b(proj, vt, seq):
    bsz, lp, _ = proj.shape
    gw = B_STREAMS * BLK
    pw = B_PART * BLK
    n_part = B_STREAMS // B_PART
    assert (A_Q + 2 * A_KV) % pw == 0 and B_W % gw == 0 and B_STREAMS % B_PART == 0
    qcol = (A_Q + 2 * A_KV) // pw
    kcol = qcol + B_W // pw

    def col_specs(rows, first, row_index):
        return [pl.BlockSpec((1, rows, pw),
                             lambda b, p, i, s=s: (b, row_index(i), first + p * n_part + s))
                for s in range(n_part)]

    return pl.pallas_call(
        functools.partial(_attn_b_kernel, seq=seq),
        grid=(bsz, B_W // gw, pl.cdiv(lp, TQ)),
        in_specs=col_specs(TQ, qcol, lambda i: i) + col_specs(lp, kcol, lambda i: 0) + [
            pl.BlockSpec((1, gw, lp), lambda b, p, i: (b, p, 0)),
        ],
        out_specs=pl.BlockSpec((1, TQ, gw), lambda b, p, i: (b, i, p)),
        out_shape=jax.ShapeDtypeStruct((bsz, lp, B_W), BF16),
        scratch_shapes=[pltpu.VMEM((B_STREAMS, 1, 2 * TQ), F32),
                        pltpu.VMEM((B_STREAMS, BLK, 2 * TQ), F32)],
        compiler_params=_params(("arbitrary", "arbitrary", "arbitrary")),
        name="attn_b",
    )(*([proj] * (2 * n_part)), vt)


def _attn_c_kernel(slopes_ref, lam_ref, g_ref, q_ref, k_ref, vt_ref, o_ref,
                   m_ref, acc_ref, *, seq, lambda_init):
    p = pl.program_id(1)
    i = pl.program_id(2)
    n_ft = seq // TQ
    w = 2 * TQ
    slope = [slopes_ref[p * NSTREAM + g] for g in range(NSTREAM)]
    qd = [_split_heads_rows(q_ref[0, :, g * BLK:(g + 1) * BLK] * SCALE)
          for g in range(NSTREAM)]
    meta = pl.ds(seq, N_META)

    def cols(g):
        return slice(g * BLK, (g + 1) * BLK)

    def vt_ones(g, rows):
        tk = rows.size
        return jnp.concatenate([vt_ref[0, cols(g), rows], jnp.ones((ACC_ROWS - BLK, tk), BF16)],
                               axis=0)

    def meta_scores(g):
        return _nt_dot(k_ref[0, meta, cols(g)], qd[g])

    rm = lax.broadcasted_iota(jnp.int32, (N_META, w), 0)
    cm = lax.broadcasted_iota(jnp.int32, (N_META, w), 1) & (TQ - 1)

    @pl.when(i == n_ft)
    def _():
        dist = jnp.abs(cm - rm).astype(F32)
        for g in range(NSTREAM):
            t = meta_scores(g) - slope[g] * dist
            m0 = jnp.max(t, axis=0, keepdims=True)
            e0 = jnp.exp(t - m0)
            m_ref[g] = m0
            acc_ref[g] = jnp.dot(vt_ones(g, meta), e0.astype(BF16), preferred_element_type=F32)

    @pl.when(i < n_ft)
    def _():
        r = lax.broadcasted_iota(jnp.int32, (TQ, w), 0)
        c = lax.broadcasted_iota(jnp.int32, (TQ, w), 1) & (TQ - 1)
        lane_k = lax.broadcasted_iota(jnp.int32, (TQ, BLK), 1)
        row_k = lax.broadcasted_iota(jnp.int32, (TQ, BLK), 0)
        key_off = jnp.where(lane_k < 3, row_k, 0).astype(F32).astype(BF16)
        lane_q = lax.broadcasted_iota(jnp.int32, (16, BLK), 1)
        qa = []
        for g in range(NSTREAM):
            sl = jnp.full((16, BLK), slope[g], F32)
            s_hi = sl.astype(BF16).astype(F32)
            s_mid = (sl - s_hi).astype(BF16).astype(F32)
            s_lo = (sl - s_hi) - s_mid
            sl3 = jnp.where(lane_q == 0, s_hi,
                            jnp.where(lane_q == 1, s_mid, jnp.where(lane_q == 2, s_lo, 0.0)))
            sl3 = jnp.concatenate([sl3.astype(BF16)] * (w // 16), axis=0)
            qa.append(jnp.concatenate([qd[g], sl3], axis=1))

        def scores(g, rows):
            return _nt_dot(jnp.concatenate([k_ref[0, rows, cols(g)], key_off], axis=1), qa[g])

        diag = pl.ds(pl.multiple_of(i * TQ, TQ), TQ)
        vis = (r >> CHUNK_SHIFT) <= (c >> CHUNK_SHIFT)
        ahead = jnp.where(vis, 2.0 * jnp.maximum(r - c, 0).astype(F32), HIDE)
        meta_off = (rm - (N_META + i * TQ)).astype(F32)
        s_d = [scores(g, diag) for g in range(NSTREAM)]
        s_m = [meta_scores(g) for g in range(NSTREAM)]
        e_d, e_m = [], []
        for g in range(NSTREAM):
            t_d = s_d[g] - slope[g] * ahead
            t_m = s_m[g] + slope[g] * meta_off
            m0 = jnp.maximum(jnp.max(t_d, axis=0, keepdims=True),
                             jnp.max(t_m, axis=0, keepdims=True))
            e_d.append(jnp.exp(t_d - m0).astype(BF16))
            e_m.append(jnp.exp(t_m - m0).astype(BF16))
            m_ref[g] = m0
        for g in range(NSTREAM):
            acc_ref[g] = (jnp.dot(vt_ones(g, diag), e_d[g], preferred_element_type=F32)
                          + jnp.dot(vt_ones(g, meta), e_m[g], preferred_element_type=F32))

        def older_tile(j):
            rows = pl.ds(pl.multiple_of(j * TQ, TQ), TQ)
            s = [scores(g, rows) for g in range(NSTREAM)]
            es, corrs = [], []
            for g in range(NSTREAM):
                cj = slope[g] * ((i - j) * TQ).astype(F32)
                m_old = m_ref[g]
                m_new = jnp.maximum(m_old, jnp.max(s[g], axis=0, keepdims=True) - cj)
                es.append(jnp.exp(s[g] - (m_new + cj)).astype(BF16))
                corrs.append(jnp.exp(m_old - m_new))
                m_ref[g] = m_new
            pvs = [jnp.dot(vt_ones(g, rows), es[g], preferred_element_type=F32)
                   for g in range(NSTREAM)]
            for g in range(NSTREAM):
                acc_ref[g] = corrs[g] * acc_ref[g] + pvs[g]

        def body(j, carry):
            older_tile(j)
            return carry

        lax.fori_loop(0, i, body, 0)

    lv = lam_ref[...]
    t1 = jnp.sum(lv[0:1] * lv[1:2], axis=-1, keepdims=True)
    t2 = jnp.sum(lv[2:3] * lv[3:4], axis=-1, keepdims=True)
    lam = jnp.exp(t1) - jnp.exp(t2) + lambda_init

    for g in range(NSTREAM):
        acc = acc_ref[g]
        o_all = acc[:BLK] / acc[BLK:BLK + 1]
        o_t = o_all[:, :TQ] - lam * o_all[:, TQ:]
        ms = jnp.mean(o_t * o_t, axis=0, keepdims=True)
        o = (o_t * lax.rsqrt(ms + EPS)).T * g_ref[...] * (1.0 - lambda_init)
        o_ref[0, :, cols(g)] = o.astype(o_ref.dtype)


def _attn_c(proj, vt, slopes, lam_vecs, subln_g, lambda_init, seq):
    bsz, lp, _ = proj.shape
    gw = NSTREAM * BLK
    kcol = C_QK // gw
    return pl.pallas_call(
        functools.partial(_attn_c_kernel, seq=seq, lambda_init=lambda_init),
        grid=(bsz, C_QK // gw, pl.cdiv(lp, TQ)),
        in_specs=[
            pl.BlockSpec(memory_space=pltpu.SMEM),
            pl.BlockSpec((4, HEAD_DIM), lambda b, p, i: (0, 0)),
            pl.BlockSpec((1, 2 * HEAD_DIM), lambda b, p, i: (0, 0)),
            pl.BlockSpec((1, TQ, gw), lambda b, p, i: (b, i, p)),
            pl.BlockSpec((1, lp, gw), lambda b, p, i: (b, 0, kcol + p)),
            pl.BlockSpec((1, gw, lp), lambda b, p, i: (b, p, 0)),
        ],
        out_specs=pl.BlockSpec((1, TQ, gw), lambda b, p, i: (b, i, p)),
        out_shape=jax.ShapeDtypeStruct((bsz, lp, C_QK), BF16),
        scratch_shapes=[pltpu.VMEM((NSTREAM, 1, 2 * TQ), F32),
                        pltpu.VMEM((NSTREAM, ACC_ROWS, 2 * TQ), F32)],
        compiler_params=_params(("arbitrary", "arbitrary", "arbitrary")),
        name="attn_c",
    )(slopes, lam_vecs, subln_g.reshape(1, 2 * HEAD_DIM), proj, proj, vt)


def _alibi_slopes(n):
    return jnp.exp2(-8.0 * (jnp.arange(n, dtype=F32) + 1.0) / n)


def kernel(x, meta_tokens, ab_norm, w_in_ab, attn_sinks, w_out_ab, c_norm, w_in_c,
           diff_lambda, diff_subln, w_out_c, mlp_norm, w_mlp_in, w_mlp_out, final_norm):
    bsz, seq, d = x.shape
    lp = seq + BLK
    assert seq % TQ == 0 and d == D_MODEL
    t = bsz * lp
    depth = mlp_norm.shape[0]

    tail = jnp.concatenate([meta_tokens.astype(x.dtype), jnp.zeros((BLK - N_META, d), x.dtype)])
    h = (x, tail)

    tm = lp // 2

    for layer in range(depth):
        li = layer // 2
        if layer % 2 == 0:
            proj = _norm_matmul(h, ab_norm[li], w_in_ab, li, relu2=False, tm=tm, tn=768)
            proj = proj.reshape(bsz, lp, AB_IN)
            vt_a = _transposed_cols(proj, A_Q + A_KV, A_KV)
            out_a = _attn_a(proj, vt_a, _alibi_slopes(A_HEADS), attn_sinks[li].astype(F32), seq)
            vt_b = _transposed_cols(proj, AB_IN - B_W, B_W)
            out_b = _attn_b(proj, vt_b, seq)
            h = _matmul_res([out_a.reshape(t, A_Q), out_b.reshape(t, B_W)], w_out_ab, li, h,
                            tm=lp, tn=512, tk=A_Q + B_W)
        else:
            lambda_init = 0.8 - 0.6 * math.exp(-0.3 * layer)
            proj = _norm_matmul(h, c_norm[li], w_in_c, li, relu2=False, tm=tm, tn=1024)
            proj = proj.reshape(bsz, lp, 3 * C_QK)
            vt_c = _transposed_cols(proj, 2 * C_QK, C_QK)
            out_c = _attn_c(proj, vt_c, _alibi_slopes(C_HEADS), diff_lambda[li].astype(F32),
                            diff_subln[li].astype(F32), lambda_init, seq)
            h = _matmul_res([out_c.reshape(t, C_QK)], w_out_c, li, h, tm=lp, tn=512, tk=2048)
        u = _norm_matmul(h, mlp_norm[layer], w_mlp_in, layer, relu2=True, tm=tm, tn=1024)
        h = _matmul_res([u], w_mlp_out, layer, h, tm=tm, tn=1024, tk=2048)

    return _final_norm(h.reshape(bsz, lp, d), final_norm, seq)
```

```python
import functools
import math

import jax
import jax.numpy as jnp
from jax import lax
from jax.experimental import pallas as pl
from jax.experimental.pallas import tpu as pltpu

D_MODEL = 2048
N_META = 16
CHUNK = 64
CHUNK_SHIFT = CHUNK.bit_length() - 1
HEAD_DIM = 64
EPS = 1e-6
HIDE = 1e32

A_HEADS = 16
A_KV_HEADS = 4
A_GROUP = A_HEADS // A_KV_HEADS
WINDOW_CHUNKS = 2
B_HEADS = 16
C_HEADS = D_MODEL // (2 * HEAD_DIM)

A_Q = A_HEADS * HEAD_DIM
A_KV = A_KV_HEADS * HEAD_DIM
B_W = B_HEADS * HEAD_DIM
AB_IN = A_Q + 2 * A_KV + 3 * B_W
C_QK = C_HEADS * 2 * HEAD_DIM

BLK = 128
TQ = 256
NSTREAM = 8
B_STREAMS = 8
B_PART = 4
ACC_ROWS = BLK + 16
SCALE = HEAD_DIM ** -0.5

V7X_VMEM_BYTES = 64 * 1024 * 1024
VMEM_LIMIT = V7X_VMEM_BYTES * 7 // 8

F32 = jnp.float32
BF16 = jnp.bfloat16


def _params(sem, vmem=VMEM_LIMIT):
    return pltpu.CompilerParams(dimension_semantics=sem, vmem_limit_bytes=vmem)


def _nt_dot(a, b):
    return lax.dot_general(a, b, (((1,), (1,)), ((), ())), preferred_element_type=F32)


def _split_heads_rows(x):
    lane = lax.broadcasted_iota(jnp.int32, x.shape, 1)
    zero = jnp.zeros_like(x)
    return jnp.concatenate([jnp.where(lane < HEAD_DIM, x, zero),
                            jnp.where(lane >= HEAD_DIM, x, zero)], axis=0)


def _norm_rows(x, g):
    ms = jnp.mean(x * x, axis=-1, keepdims=True)
    return (x * lax.rsqrt(ms + EPS) * g).astype(BF16)


def _norm_matmul_kernel(*refs, relu2, split):
    x_ref, tail_ref = (refs[0], refs[1]) if split else (refs[0], None)
    g_ref, w_ref, o_ref, hn_ref = refs[-4:]

    @pl.when(pl.program_id(1) == 0)
    def _():
        hn_ref[...] = _norm_rows(x_ref[0] if split else x_ref[...], g_ref[...])
        if split:
            @pl.when(pl.program_id(0) % 2 == 1)
            def _():
                hn_ref[hn_ref.shape[0] - BLK:, :] = _norm_rows(tail_ref[...], g_ref[...])

    acc = jnp.dot(hn_ref[...], w_ref[...].astype(BF16), preferred_element_type=F32)
    if relu2:
        acc = jnp.square(jnp.maximum(acc, 0.0))
    o_ref[...] = acc.astype(o_ref.dtype)


def _norm_matmul(x, g, w_stack, li, *, relu2, tm, tn):
    split = isinstance(x, tuple)
    if split:
        frames, tail = x
        bsz, seq, k = frames.shape
        t = bsz * (seq + BLK)
        assert 2 * tm == seq + BLK
        x_args = [frames, tail]
        x_specs = [pl.BlockSpec((1, tm, k), lambda i, j: (i // 2, i % 2, 0)),
                   pl.BlockSpec((BLK, k), lambda i, j: (0, 0))]
    else:
        t, k = x.shape
        x_args = [x]
        x_specs = [pl.BlockSpec((tm, k), lambda i, j: (i, 0))]
    n = w_stack.shape[2]
    assert t % tm == 0 and n % tn == 0
    return pl.pallas_call(
        functools.partial(_norm_matmul_kernel, relu2=relu2, split=split),
        grid=(t // tm, n // tn),
        in_specs=x_specs + [
            pl.BlockSpec((1, k), lambda i, j: (0, 0)),
            pl.BlockSpec((None, k, tn), lambda i, j: (li, 0, j)),
        ],
        out_specs=pl.BlockSpec((tm, tn), lambda i, j: (i, j)),
        out_shape=jax.ShapeDtypeStruct((t, n), BF16),
        scratch_shapes=[pltpu.VMEM((tm, k), BF16)],
        compiler_params=_params(("arbitrary", "arbitrary")),
        name="norm_matmul_relu2" if relu2 else "norm_matmul",
    )(*x_args, g.reshape(1, k), w_stack)


def _matmul_res_kernel(*refs, n_a, split):
    a_refs, w_ref, r_refs, o_ref = refs[:n_a], refs[n_a], refs[n_a + 1:-1], refs[-1]
    w = w_ref[...].astype(BF16)
    ka = w.shape[0] // n_a
    part = jnp.dot(a_refs[0][...], w[:ka], preferred_element_type=F32)
    for s in range(1, n_a):
        part += jnp.dot(a_refs[s][...], w[s * ka:(s + 1) * ka], preferred_element_type=F32)

    @pl.when(pl.program_id(2) == 0)
    def _():
        if split:
            res = jnp.concatenate([r_refs[0][0], r_refs[1][...]], axis=0)
        else:
            res = r_refs[0][...]
        o_ref[...] = res + part

    @pl.when(pl.program_id(2) != 0)
    def _():
        o_ref[...] += part


def _matmul_res(a_parts, w_stack, li, res, *, tm, tn, tk):
    n_a = len(a_parts)
    t = a_parts[0].shape[0]
    _, k, n = w_stack.shape
    split = isinstance(res, tuple)
    if split:
        assert tm == res[0].shape[1] + BLK
        r_args = list(res)
        r_specs = [pl.BlockSpec((1, tm - BLK, tn), lambda i, j, kk: (i, 0, j)),
                   pl.BlockSpec((BLK, tn), lambda i, j, kk: (0, j))]
    else:
        r_args = [res]
        r_specs = [pl.BlockSpec((tm, tn), lambda i, j, kk: (i, j))]
    assert sum(a.shape[1] for a in a_parts) == k and (n_a == 1 or tk == k)
    assert t % tm == 0 and n % tn == 0 and k % tk == 0
    return pl.pallas_call(
        functools.partial(_matmul_res_kernel, n_a=n_a, split=split),
        grid=(t // tm, n // tn, k // tk),
        in_specs=[pl.BlockSpec((tm, tk // n_a), lambda i, j, kk: (i, kk)) for _ in a_parts] + [
            pl.BlockSpec((None, tk, tn), lambda i, j, kk: (li, kk, j))] + r_specs,
        out_specs=pl.BlockSpec((tm, tn), lambda i, j, kk: (i, j)),
        out_shape=jax.ShapeDtypeStruct((t, n), F32),
        compiler_params=_params(("arbitrary", "arbitrary", "arbitrary")),
        name="matmul_res",
    )(*a_parts, w_stack, *r_args)


def _final_norm_kernel(x_ref, g_ref, o_ref):
    x = x_ref[0]
    ms = jnp.mean(x * x, axis=-1, keepdims=True)
    o_ref[0] = x * lax.rsqrt(ms + EPS) * g_ref[...]


def _final_norm(h3, g, seq):
    bsz, lp, d = h3.shape
    rows = 4 * BLK
    assert seq % rows == 0
    return pl.pallas_call(
        _final_norm_kernel,
        grid=(bsz, seq // rows),
        in_specs=[
            pl.BlockSpec((1, rows, d), lambda b, r: (b, r, 0)),
            pl.BlockSpec((1, d), lambda b, r: (0, 0)),
        ],
        out_specs=pl.BlockSpec((1, rows, d), lambda b, r: (b, r, 0)),
        out_shape=jax.ShapeDtypeStruct((bsz, seq, d), F32),
        compiler_params=_params(("arbitrary", "arbitrary")),
        name="final_norm",
    )(h3, g.reshape(1, d))


def _transpose_kernel(x_ref, o_ref):
    for c in range(x_ref.shape[2] // BLK):
        o_ref[0, c * BLK:(c + 1) * BLK, :] = x_ref[0, :, c * BLK:(c + 1) * BLK].T


def _transposed_cols(proj, col0, ncols):
    bsz, lp, _ = proj.shape
    cw = math.gcd(col0, ncols, 8 * BLK)
    assert cw % BLK == 0
    return pl.pallas_call(
        _transpose_kernel,
        grid=(bsz, ncols // cw),
        in_specs=[pl.BlockSpec((1, lp, cw), lambda b, c: (b, 0, col0 // cw + c))],
        out_specs=pl.BlockSpec((1, cw, lp), lambda b, c: (b, c, 0)),
        out_shape=jax.ShapeDtypeStruct((bsz, ncols, lp), proj.dtype),
        compiler_params=_params(("arbitrary", "arbitrary")),
        name="transpose_v",
    )(proj)


A_BAND = TQ + BLK
A_VROWS = HEAD_DIM + 16


def _attn_a_kernel(slopes_ref, sinks_ref, q_ref, k_ref, vt_ref, o_ref, *, seq):
    i = pl.program_id(1)
    lp = k_ref.shape[1]
    wl = A_GROUP * TQ
    start = pl.multiple_of(jnp.clip(i * TQ - BLK, 0, lp - A_BAND), BLK)
    band = pl.ds(start, A_BAND)
    meta = pl.ds(seq, N_META)

    def positions(nrows):
        r = lax.broadcasted_iota(jnp.int32, (nrows, TQ), 0)
        u_q = i * TQ + lax.broadcasted_iota(jnp.int32, (nrows, TQ), 1)
        q_frame = u_q < seq
        return (r, jnp.where(q_frame, u_q + N_META, u_q - seq),
                jnp.where(q_frame, 1 + (u_q >> CHUNK_SHIFT), 0))

    r, pos_q, qc = positions(A_BAND)
    u_k = start + r
    kc = 1 + (u_k >> CHUNK_SHIFT)
    vis = (u_k < seq) & (kc <= qc) & (kc >= qc - WINDOW_CHUNKS)
    dist = jnp.where(vis, jnp.abs(pos_q - (u_k + N_META)).astype(F32), HIDE)
    rm, pos_qm, _ = positions(N_META)
    dist_m = jnp.abs(pos_qm - rm).astype(F32)
    head_lane = lax.broadcasted_iota(jnp.int32, (1, wl), 1) >> (TQ.bit_length() - 1)

    def per_head(ref, g):
        v = jnp.zeros((1, wl), F32)
        for a in range(A_GROUP):
            v = jnp.where(head_lane == a, ref[g * A_GROUP + a], v)
        return v

    def dup(kt, g):
        kg = kt[:, g * HEAD_DIM:(g + 1) * HEAD_DIM]
        return jnp.concatenate([kg, kg], axis=1)

    groups = range(A_KV_HEADS)
    k_band, k_meta = k_ref[0, band, :], k_ref[0, meta, :]
    qd = [jnp.concatenate(
        [_split_heads_rows(q_ref[0, :, (2 * g + p) * BLK:(2 * g + p + 1) * BLK] * SCALE)
         for p in range(2)], axis=0) for g in groups]
    s_b = [_nt_dot(dup(k_band, g), qd[g]) for g in groups]
    s_m = [_nt_dot(dup(k_meta, g), qd[g]) for g in groups]
    e_b, e_m, extra = [], [], []
    for g in groups:
        slope, sink = per_head(slopes_ref, g), per_head(sinks_ref, g)
        t_b = s_b[g] - slope * jnp.concatenate([dist] * A_GROUP, axis=1)
        t_m = s_m[g] - slope * jnp.concatenate([dist_m] * A_GROUP, axis=1)
        mx = jnp.maximum(jnp.maximum(jnp.max(t_b, axis=0, keepdims=True),
                                     jnp.max(t_m, axis=0, keepdims=True)), sink)
        e_b.append(jnp.exp(t_b - mx).astype(BF16))
        e_m.append(jnp.exp(t_m - mx).astype(BF16))
        extra.append(jnp.exp(sink - mx))

    def vt_ones(g, rows):
        return jnp.concatenate([vt_ref[0, g * HEAD_DIM:(g + 1) * HEAD_DIM, rows],
                                jnp.ones((A_VROWS - HEAD_DIM, rows.size), BF16)], axis=0)

    acc = [jnp.dot(vt_ones(g, band), e_b[g], preferred_element_type=F32)
           + jnp.dot(vt_ones(g, meta), e_m[g], preferred_element_type=F32) for g in groups]
    for g in groups:
        o_t = acc[g][:HEAD_DIM] / (acc[g][HEAD_DIM:HEAD_DIM + 1] + extra[g])
        for p in range(2):
            pair = jnp.concatenate([o_t[:, (2 * p) * TQ:(2 * p + 1) * TQ],
                                    o_t[:, (2 * p + 1) * TQ:(2 * p + 2) * TQ]], axis=0)
            o_ref[0, :, (2 * g + p) * BLK:(2 * g + p + 1) * BLK] = pair.T.astype(o_ref.dtype)


def _attn_a(proj, vt, slopes, sinks, seq):
    bsz, lp, _ = proj.shape
    kcol = A_Q // A_KV
    return pl.pallas_call(
        functools.partial(_attn_a_kernel, seq=seq),
        grid=(bsz, pl.cdiv(lp, TQ)),
        in_specs=[
            pl.BlockSpec(memory_space=pltpu.SMEM),
            pl.BlockSpec(memory_space=pltpu.SMEM),
            pl.BlockSpec((1, TQ, A_Q), lambda b, i: (b, i, 0)),
            pl.BlockSpec((1, lp, A_KV), lambda b, i: (b, 0, kcol)),
            pl.BlockSpec((1, A_KV, lp), lambda b, i: (b, 0, 0)),
        ],
        out_specs=pl.BlockSpec((1, TQ, A_Q), lambda b, i: (b, i, 0)),
        out_shape=jax.ShapeDtypeStruct((bsz, lp, A_Q), BF16),
        compiler_params=_params(("arbitrary", "arbitrary")),
        name="attn_a",
    )(slopes, sinks, proj, proj, vt)


def _attn_b_kernel(*refs, seq):
    n_part = B_STREAMS // B_PART
    q_refs, k_refs = refs[:n_part], refs[n_part:2 * n_part]
    vt_ref, o_ref, run_ref, acc_ref = refs[2 * n_part:]
    i = pl.program_id(2)
    n_ft = seq // TQ
    w = 2 * TQ

    def part(part_refs, g, rows):
        lo = (g % B_PART) * BLK
        return part_refs[g // B_PART][0, rows, lo:lo + BLK]

    streams = range(B_STREAMS)
    qd = [_split_heads_rows(part(q_refs, g, slice(None)) * SCALE)
          for g in streams]
    run_ref[...] = jnp.zeros_like(run_ref)
    acc_ref[...] = jnp.zeros_like(acc_ref)

    def tri(tk):
        s_idx = lax.broadcasted_iota(jnp.int32, (tk + 16, tk), 0)
        j_idx = lax.broadcasted_iota(jnp.int32, (tk + 16, tk), 1)
        return jnp.where((j_idx > s_idx) | (s_idx >= tk), 1.0, 0.0).astype(BF16)

    def tile(rows, tri_op, mask):
        zs = [_nt_dot(part(k_refs, g, rows), qd[g]) for g in streams]
        tk = rows.size
        sps, newers = [], []
        for g in streams:
            sp = jnp.maximum(zs[g], 0.0) + jnp.log(1.0 + jnp.exp(-jnp.abs(zs[g])))
            if mask is not None:
                sp = jnp.where(mask, sp, 0.0)
            newers.append(jnp.dot(tri_op, sp.astype(BF16), preferred_element_type=F32))
            sps.append(sp)
        pvs = []
        for g in streams:
            wgt = jnp.exp(zs[g] - (sps[g] + newers[g][:tk] + run_ref[g]))
            if mask is not None:
                wgt = jnp.where(mask, wgt, 0.0)
            pvs.append(jnp.dot(vt_ref[0, g * BLK:(g + 1) * BLK, rows], wgt.astype(BF16),
                               preferred_element_type=F32))
        for g in streams:
            acc_ref[g] += pvs[g]
            run_ref[g] += newers[g][tk:tk + 1]

    @pl.when(i < n_ft)
    def _():
        tri_tq = tri(TQ)
        r = lax.broadcasted_iota(jnp.int32, (TQ, w), 0)
        c = lax.broadcasted_iota(jnp.int32, (TQ, w), 1) & (TQ - 1)
        tile(pl.ds(pl.multiple_of(i * TQ, TQ), TQ), tri_tq, r < c)

        def body(jj, carry):
            tile(pl.ds(pl.multiple_of((i - 1 - jj) * TQ, TQ), TQ), tri_tq, None)
            return carry

        lax.fori_loop(0, i, body, 0)

    r = lax.broadcasted_iota(jnp.int32, (N_META, w), 0)
    c = lax.broadcasted_iota(jnp.int32, (N_META, w), 1) & (TQ - 1)
    tile(pl.ds(seq, N_META), tri(N_META), (i < n_ft) | (r < c))

    for g in streams:
        acc = acc_ref[g]
        o_t = jnp.concatenate([acc[:HEAD_DIM, :TQ], acc[HEAD_DIM:, TQ:]], axis=0)
        o_ref[0, :, g * BLK:(g + 1) * BLK] = o_t.T.astype(o_ref.dtype)


def _attn_b(proj, vt, seq):
    bsz, lp, _ = proj.shape
    gw = B_STREAMS * BLK
    pw = B_PART * BLK
    n_part = B_STREAMS // B_PART
    assert (A_Q + 2 * A_KV) % pw == 0 and B_W % gw == 0 and B_STREAMS % B_PART == 0
    qcol = (A_Q + 2 * A_KV) // pw
    kcol = qcol + B_W // pw

    def col_specs(rows, first, row_index):
        return [pl.BlockSpec((1, rows, pw),
                             lambda b, p, i, s=s: (b, row_index(i), first + p * n_part + s))
                for s in range(n_part)]

    return pl.pallas_call(
        functools.partial(_attn_b_kernel, seq=seq),
        grid=(bsz, B_W // gw, pl.cdiv(lp, TQ)),
        in_specs=col_specs(TQ, qcol, lambda i: i) + col_specs(lp, kcol, lambda i: 0) + [
            pl.BlockSpec((1, gw, lp), lambda b, p, i: (b, p, 0)),
        ],
        out_specs=pl.BlockSpec((1, TQ, gw), lambda b, p, i: (b, i, p)),
        out_shape=jax.ShapeDtypeStruct((bsz, lp, B_W), BF16),
        scratch_shapes=[pltpu.VMEM((B_STREAMS, 1, 2 * TQ), F32),
                        pltpu.VMEM((B_STREAMS, BLK, 2 * TQ), F32)],
        compiler_params=_params(("arbitrary", "arbitrary", "arbitrary")),
        name="attn_b",
    )(*([proj] * (2 * n_part)), vt)


def _attn_c_kernel(slopes_ref, lam_ref, g_ref, q_ref, k_ref, vt_ref, o_ref,
                   m_ref, acc_ref, *, seq, lambda_init):
    p = pl.program_id(1)
    i = pl.program_id(2)
    n_ft = seq // TQ
    w = 2 * TQ
    slope = [slopes_ref[p * NSTREAM + g] for g in range(NSTREAM)]
    qd = [_split_heads_rows(q_ref[0, :, g * BLK:(g + 1) * BLK] * SCALE)
          for g in range(NSTREAM)]
    meta = pl.ds(seq, N_META)

    def cols(g):
        return slice(g * BLK, (g + 1) * BLK)

    def vt_ones(g, rows):
        tk = rows.size
        return jnp.concatenate([vt_ref[0, cols(g), rows], jnp.ones((ACC_ROWS - BLK, tk), BF16)],
                               axis=0)

    def meta_scores(g):
        return _nt_dot(k_ref[0, meta, cols(g)], qd[g])

    rm = lax.broadcasted_iota(jnp.int32, (N_META, w), 0)
    cm = lax.broadcasted_iota(jnp.int32, (N_META, w), 1) & (TQ - 1)

    @pl.when(i == n_ft)
    def _():
        dist = jnp.abs(cm - rm).astype(F32)
        for g in range(NSTREAM):
            t = meta_scores(g) - slope[g] * dist
            m0 = jnp.max(t, axis=0, keepdims=True)
            e0 = jnp.exp(t - m0)
            m_ref[g] = m0
            acc_ref[g] = jnp.dot(vt_ones(g, meta), e0.astype(BF16), preferred_element_type=F32)

    @pl.when(i < n_ft)
    def _():
        r = lax.broadcasted_iota(jnp.int32, (TQ, w), 0)
        c = lax.broadcasted_iota(jnp.int32, (TQ, w), 1) & (TQ - 1)
        lane_k = lax.broadcasted_iota(jnp.int32, (TQ, BLK), 1)
        row_k = lax.broadcasted_iota(jnp.int32, (TQ, BLK), 0)
        key_off = jnp.where(lane_k < 3, row_k, 0).astype(F32).astype(BF16)
        lane_q = lax.broadcasted_iota(jnp.int32, (16, BLK), 1)
        qa = []
        for g in range(NSTREAM):
            sl = jnp.full((16, BLK), slope[g], F32)
            s_hi = sl.astype(BF16).astype(F32)
            s_mid = (sl - s_hi).astype(BF16).astype(F32)
            s_lo = (sl - s_hi) - s_mid
            sl3 = jnp.where(lane_q == 0, s_hi,
                            jnp.where(lane_q == 1, s_mid, jnp.where(lane_q == 2, s_lo, 0.0)))
            sl3 = jnp.concatenate([sl3.astype(BF16)] * (w // 16), axis=0)
            qa.append(jnp.concatenate([qd[g], sl3], axis=1))

        def scores(g, rows):
            return _nt_dot(jnp.concatenate([k_ref[0, rows, cols(g)], key_off], axis=1), qa[g])

        diag = pl.ds(pl.multiple_of(i * TQ, TQ), TQ)
        vis = (r >> CHUNK_SHIFT) <= (c >> CHUNK_SHIFT)
        ahead = jnp.where(vis, 2.0 * jnp.maximum(r - c, 0).astype(F32), HIDE)
        meta_off = (rm - (N_META + i * TQ)).astype(F32)
        s_d = [scores(g, diag) for g in range(NSTREAM)]
        s_m = [meta_scores(g) for g in range(NSTREAM)]
        e_d, e_m = [], []
        for g in range(NSTREAM):
            t_d = s_d[g] - slope[g] * ahead
            t_m = s_m[g] + slope[g] * meta_off
            m0 = jnp.maximum(jnp.max(t_d, axis=0, keepdims=True),
                             jnp.max(t_m, axis=0, keepdims=True))
            e_d.append(jnp.exp(t_d - m0).astype(BF16))
            e_m.append(jnp.exp(t_m - m0).astype(BF16))
            m_ref[g] = m0
        for g in range(NSTREAM):
            acc_ref[g] = (jnp.dot(vt_ones(g, diag), e_d[g], preferred_element_type=F32)
                          + jnp.dot(vt_ones(g, meta), e_m[g], preferred_element_type=F32))

        def older_tile(j):
            rows = pl.ds(pl.multiple_of(j * TQ, TQ), TQ)
            s = [scores(g, rows) for g in range(NSTREAM)]
            es, corrs = [], []
            for g in range(NSTREAM):
                cj = slope[g] * ((i - j) * TQ).astype(F32)
                m_old = m_ref[g]
                m_new = jnp.maximum(m_old, jnp.max(s[g], axis=0, keepdims=True) - cj)
                es.append(jnp.exp(s[g] - (m_new + cj)).astype(BF16))
                corrs.append(jnp.exp(m_old - m_new))
                m_ref[g] = m_new
            pvs = [jnp.dot(vt_ones(g, rows), es[g], preferred_element_type=F32)
                   for g in range(NSTREAM)]
            for g in range(NSTREAM):
                acc_ref[g] = corrs[g] * acc_ref[g] + pvs[g]

        def body(j, carry):
            older_tile(j)
            return carry

        lax.fori_loop(0, i, body, 0)

    lv = lam_ref[...]
    t1 = jnp.sum(lv[0:1] * lv[1:2], axis=-1, keepdims=True)
    t2 = jnp.sum(lv[2:3] * lv[3:4], axis=-1, keepdims=True)
    lam = jnp.exp(t1) - jnp.exp(t2) + lambda_init

    for g in range(NSTREAM):
        acc = acc_ref[g]
        o_all = acc[:BLK] / acc[BLK:BLK + 1]
        o_t = o_all[:, :TQ] - lam * o_all[:, TQ:]
        ms = jnp.mean(o_t * o_t, axis=0, keepdims=True)
        o = (o_t * lax.rsqrt(ms + EPS)).T * g_ref[...] * (1.0 - lambda_init)
        o_ref[0, :, cols(g)] = o.astype(o_ref.dtype)


def _attn_c(proj, vt, slopes, lam_vecs, subln_g, lambda_init, seq):
    bsz, lp, _ = proj.shape
    gw = NSTREAM * BLK
    kcol = C_QK // gw
    return pl.pallas_call(
        functools.partial(_attn_c_kernel, seq=seq, lambda_init=lambda_init),
        grid=(bsz, C_QK // gw, pl.cdiv(lp, TQ)),
        in_specs=[
            pl.BlockSpec(memory_space=pltpu.SMEM),
            pl.BlockSpec((4, HEAD_DIM), lambda b, p, i: (0, 0)),
            pl.BlockSpec((1, 2 * HEAD_DIM), lambda b, p, i: (0, 0)),
            pl.BlockSpec((1, TQ, gw), lambda b, p, i: (b, i, p)),
            pl.BlockSpec((1, lp, gw), lambda b, p, i: (b, 0, kcol + p)),
            pl.BlockSpec((1, gw, lp), lambda b, p, i: (b, p, 0)),
        ],
        out_specs=pl.BlockSpec((1, TQ, gw), lambda b, p, i: (b, i, p)),
        out_shape=jax.ShapeDtypeStruct((bsz, lp, C_QK), BF16),
        scratch_shapes=[pltpu.VMEM((NSTREAM, 1, 2 * TQ), F32),
                        pltpu.VMEM((NSTREAM, ACC_ROWS, 2 * TQ), F32)],
        compiler_params=_params(("arbitrary", "arbitrary", "arbitrary")),
        name="attn_c",
    )(slopes, lam_vecs, subln_g.reshape(1, 2 * HEAD_DIM), proj, proj, vt)


def _alibi_slopes(n):
    return jnp.exp2(-8.0 * (jnp.arange(n, dtype=F32) + 1.0) / n)


def kernel(x, meta_tokens, ab_norm, w_in_ab, attn_sinks, w_out_ab, c_norm, w_in_c,
           diff_lambda, diff_subln, w_out_c, mlp_norm, w_mlp_in, w_mlp_out, final_norm):
    bsz, seq, d = x.shape
    lp = seq + BLK
    assert seq % TQ == 0 and d == D_MODEL
    t = bsz * lp
    depth = mlp_norm.shape[0]

    tail = jnp.concatenate([meta_tokens.astype(x.dtype), jnp.zeros((BLK - N_META, d), x.dtype)])
    h = (x, tail)

    tm = lp // 2

    for layer in range(depth):
        li = layer // 2
        if layer % 2 == 0:
            proj = _norm_matmul(h, ab_norm[li], w_in_ab, li, relu2=False, tm=tm, tn=768)
            proj = proj.reshape(bsz, lp, AB_IN)
            vt_a = _transposed_cols(proj, A_Q + A_KV, A_KV)
            out_a = _attn_a(proj, vt_a, _alibi_slopes(A_HEADS), attn_sinks[li].astype(F32), seq)
            vt_b = _transposed_cols(proj, AB_IN - B_W, B_W)
            out_b = _attn_b(proj, vt_b, seq)
            h = _matmul_res([out_a.reshape(t, A_Q), out_b.reshape(t, B_W)], w_out_ab, li, h,
                            tm=lp, tn=512, tk=A_Q + B_W)
        else:
            lambda_init = 0.8 - 0.6 * math.exp(-0.3 * layer)
            proj = _norm_matmul(h, c_norm[li], w_in_c, li, relu2=False, tm=tm, tn=1024)
            proj = proj.reshape(bsz, lp, 3 * C_QK)
            vt_c = _transposed_cols(proj, 2 * C_QK, C_QK)
            out_c = _attn_c(proj, vt_c, _alibi_slopes(C_HEADS), diff_lambda[li].astype(F32),
                            diff_subln[li].astype(F32), lambda_init, seq)
            h = _matmul_res([out_c.reshape(t, C_QK)], w_out_c, li, h, tm=lp, tn=512, tk=2048)
        u = _norm_matmul(h, mlp_norm[layer], w_mlp_in, layer, relu2=True, tm=tm, tn=1024)
        h = _matmul_res([u], w_mlp_out, layer, h, tm=tm, tn=512, tk=4096)

    return _final_norm(h.reshape(bsz, lp, d), final_norm, seq)
```

```python
import functools
import math

import jax
import jax.numpy as jnp
from jax import lax
from jax.experimental import pallas as pl
from jax.experimental.pallas import tpu as pltpu

D_MODEL = 2048
N_META = 16
CHUNK = 64
CHUNK_SHIFT = CHUNK.bit_length() - 1
HEAD_DIM = 64
EPS = 1e-6
HIDE = 1e32

A_HEADS = 16
A_KV_HEADS = 4
A_GROUP = A_HEADS // A_KV_HEADS
WINDOW_CHUNKS = 2
B_HEADS = 16
C_HEADS = D_MODEL // (2 * HEAD_DIM)

A_Q = A_HEADS * HEAD_DIM
A_KV = A_KV_HEADS * HEAD_DIM
B_W = B_HEADS * HEAD_DIM
AB_IN = A_Q + 2 * A_KV + 3 * B_W
C_QK = C_HEADS * 2 * HEAD_DIM

BLK = 128
TQ = 256
NSTREAM = 8
B_STREAMS = 8
B_PART = 4
ACC_ROWS = BLK + 16
SCALE = HEAD_DIM ** -0.5

V7X_VMEM_BYTES = 64 * 1024 * 1024
VMEM_LIMIT = V7X_VMEM_BYTES * 7 // 8

F32 = jnp.float32
BF16 = jnp.bfloat16


def _params(sem, vmem=VMEM_LIMIT):
    return pltpu.CompilerParams(dimension_semantics=sem, vmem_limit_bytes=vmem)


def _nt_dot(a, b):
    return lax.dot_general(a, b, (((1,), (1,)), ((), ())), preferred_element_type=F32)


def _split_heads_rows(x):
    lane = lax.broadcasted_iota(jnp.int32, x.shape, 1)
    zero = jnp.zeros_like(x)
    return jnp.concatenate([jnp.where(lane < HEAD_DIM, x, zero),
                            jnp.where(lane >= HEAD_DIM, x, zero)], axis=0)


def _norm_rows(x, g):
    ms = jnp.mean(x * x, axis=-1, keepdims=True)
    return (x * lax.rsqrt(ms + EPS) * g).astype(BF16)


def _norm_matmul_kernel(*refs, relu2, split):
    x_ref, tail_ref = (refs[0], refs[1]) if split else (refs[0], None)
    g_ref, w_ref, o_ref, hn_ref = refs[-4:]

    @pl.when(pl.program_id(1) == 0)
    def _():
        hn_ref[...] = _norm_rows(x_ref[0] if split else x_ref[...], g_ref[...])
        if split:
            @pl.when(pl.program_id(0) % 2 == 1)
            def _():
                hn_ref[hn_ref.shape[0] - BLK:, :] = _norm_rows(tail_ref[...], g_ref[...])

    acc = jnp.dot(hn_ref[...], w_ref[...].astype(BF16), preferred_element_type=F32)
    if relu2:
        acc = jnp.square(jnp.maximum(acc, 0.0))
    o_ref[...] = acc.astype(o_ref.dtype)


def _norm_matmul(x, g, w_stack, li, *, relu2, tm, tn):
    split = isinstance(x, tuple)
    if split:
        frames, tail = x
        bsz, seq, k = frames.shape
        t = bsz * (seq + BLK)
        assert 2 * tm == seq + BLK
        x_args = [frames, tail]
        x_specs = [pl.BlockSpec((1, tm, k), lambda i, j: (i // 2, i % 2, 0)),
                   pl.BlockSpec((BLK, k), lambda i, j: (0, 0))]
    else:
        t, k = x.shape
        x_args = [x]
        x_specs = [pl.BlockSpec((tm, k), lambda i, j: (i, 0))]
    n = w_stack.shape[2]
    assert t % tm == 0 and n % tn == 0
    return pl.pallas_call(
        functools.partial(_norm_matmul_kernel, relu2=relu2, split=split),
        grid=(t // tm, n // tn),
        in_specs=x_specs + [
            pl.BlockSpec((1, k), lambda i, j: (0, 0)),
            pl.BlockSpec((None, k, tn), lambda i, j: (li, 0, j)),
        ],
        out_specs=pl.BlockSpec((tm, tn), lambda i, j: (i, j)),
        out_shape=jax.ShapeDtypeStruct((t, n), BF16),
        scratch_shapes=[pltpu.VMEM((tm, k), BF16)],
        compiler_params=_params(("arbitrary", "arbitrary")),
        name="norm_matmul_relu2" if relu2 else "norm_matmul",
    )(*x_args, g.reshape(1, k), w_stack)


def _matmul_res_kernel(*refs, n_a, split):
    a_refs, w_ref, r_refs, o_ref = refs[:n_a], refs[n_a], refs[n_a + 1:-1], refs[-1]
    w = w_ref[...].astype(BF16)
    ka = w.shape[0] // n_a
    part = jnp.dot(a_refs[0][...], w[:ka], preferred_element_type=F32)
    for s in range(1, n_a):
        part += jnp.dot(a_refs[s][...], w[s * ka:(s + 1) * ka], preferred_element_type=F32)

    @pl.when(pl.program_id(2) == 0)
    def _():
        if split:
            res = jnp.concatenate([r_refs[0][0], r_refs[1][...]], axis=0)
        else:
            res = r_refs[0][...]
        o_ref[...] = res + part

    @pl.when(pl.program_id(2) != 0)
    def _():
        o_ref[...] += part


def _matmul_res(a_parts, w_stack, li, res, *, tm, tn, tk):
    n_a = len(a_parts)
    t = a_parts[0].shape[0]
    _, k, n = w_stack.shape
    split = isinstance(res, tuple)
    if split:
        assert tm == res[0].shape[1] + BLK
        r_args = list(res)
        r_specs = [pl.BlockSpec((1, tm - BLK, tn), lambda i, j, kk: (i, 0, j)),
                   pl.BlockSpec((BLK, tn), lambda i, j, kk: (0, j))]
    else:
        r_args = [res]
        r_specs = [pl.BlockSpec((tm, tn), lambda i, j, kk: (i, j))]
    assert sum(a.shape[1] for a in a_parts) == k and (n_a == 1 or tk == k)
    assert t % tm == 0 and n % tn == 0 and k % tk == 0
    return pl.pallas_call(
        functools.partial(_matmul_res_kernel, n_a=n_a, split=split),
        grid=(t // tm, n // tn, k // tk),
        in_specs=[pl.BlockSpec((tm, tk // n_a), lambda i, j, kk: (i, kk)) for _ in a_parts] + [
            pl.BlockSpec((None, tk, tn), lambda i, j, kk: (li, kk, j))] + r_specs,
        out_specs=pl.BlockSpec((tm, tn), lambda i, j, kk: (i, j)),
        out_shape=jax.ShapeDtypeStruct((t, n), F32),
        compiler_params=_params(("arbitrary", "arbitrary", "arbitrary")),
        name="matmul_res",
    )(*a_parts, w_stack, *r_args)


def _final_norm_kernel(x_ref, g_ref, o_ref):
    x = x_ref[0]
    ms = jnp.mean(x * x, axis=-1, keepdims=True)
    o_ref[0] = x * lax.rsqrt(ms + EPS) * g_ref[...]


def _final_norm(h3, g, seq):
    bsz, lp, d = h3.shape
    rows = 4 * BLK
    assert seq % rows == 0
    return pl.pallas_call(
        _final_norm_kernel,
        grid=(bsz, seq // rows),
        in_specs=[
            pl.BlockSpec((1, rows, d), lambda b, r: (b, r, 0)),
            pl.BlockSpec((1, d), lambda b, r: (0, 0)),
        ],
        out_specs=pl.BlockSpec((1, rows, d), lambda b, r: (b, r, 0)),
        out_shape=jax.ShapeDtypeStruct((bsz, seq, d), F32),
        compiler_params=_params(("arbitrary", "arbitrary")),
        name="final_norm",
    )(h3, g.reshape(1, d))


def _transpose_kernel(x_ref, o_ref):
    for c in range(x_ref.shape[2] // BLK):
        o_ref[0, c * BLK:(c + 1) * BLK, :] = x_ref[0, :, c * BLK:(c + 1) * BLK].T


def _transposed_cols(proj, col0, ncols):
    bsz, lp, _ = proj.shape
    cw = math.gcd(col0, ncols, 8 * BLK)
    assert cw % BLK == 0
    return pl.pallas_call(
        _transpose_kernel,
        grid=(bsz, ncols // cw),
        in_specs=[pl.BlockSpec((1, lp, cw), lambda b, c: (b, 0, col0 // cw + c))],
        out_specs=pl.BlockSpec((1, cw, lp), lambda b, c: (b, c, 0)),
        out_shape=jax.ShapeDtypeStruct((bsz, ncols, lp), proj.dtype),
        compiler_params=_params(("arbitrary", "arbitrary")),
        name="transpose_v",
    )(proj)


A_BAND = TQ + BLK
A_VROWS = HEAD_DIM + 16


def _attn_a_kernel(slopes_ref, sinks_ref, q_ref, k_ref, vt_ref, o_ref, *, seq):
    i = pl.program_id(1)
    lp = k_ref.shape[1]
    wl = A_GROUP * TQ
    start = pl.multiple_of(jnp.clip(i * TQ - BLK, 0, lp - A_BAND), BLK)
    band = pl.ds(start, A_BAND)
    meta = pl.ds(seq, N_META)

    def positions(nrows):
        r = lax.broadcasted_iota(jnp.int32, (nrows, TQ), 0)
        u_q = i * TQ + lax.broadcasted_iota(jnp.int32, (nrows, TQ), 1)
        q_frame = u_q < seq
        return (r, jnp.where(q_frame, u_q + N_META, u_q - seq),
                jnp.where(q_frame, 1 + (u_q >> CHUNK_SHIFT), 0))

    r, pos_q, qc = positions(A_BAND)
    u_k = start + r
    kc = 1 + (u_k >> CHUNK_SHIFT)
    vis = (u_k < seq) & (kc <= qc) & (kc >= qc - WINDOW_CHUNKS)
    dist = jnp.where(vis, jnp.abs(pos_q - (u_k + N_META)).astype(F32), HIDE)
    rm, pos_qm, _ = positions(N_META)
    dist_m = jnp.abs(pos_qm - rm).astype(F32)
    head_lane = lax.broadcasted_iota(jnp.int32, (1, wl), 1) >> (TQ.bit_length() - 1)

    def per_head(ref, g):
        v = jnp.zeros((1, wl), F32)
        for a in range(A_GROUP):
            v = jnp.where(head_lane == a, ref[g * A_GROUP + a], v)
        return v

    def dup(kt, g):
        kg = kt[:, g * HEAD_DIM:(g + 1) * HEAD_DIM]
        return jnp.concatenate([kg, kg], axis=1)

    groups = range(A_KV_HEADS)
    k_band, k_meta = k_ref[0, band, :], k_ref[0, meta, :]
    qd = [jnp.concatenate(
        [_split_heads_rows(q_ref[0, :, (2 * g + p) * BLK:(2 * g + p + 1) * BLK] * SCALE)
         for p in range(2)], axis=0) for g in groups]
    s_b = [_nt_dot(dup(k_band, g), qd[g]) for g in groups]
    s_m = [_nt_dot(dup(k_meta, g), qd[g]) for g in groups]
    e_b, e_m, extra = [], [], []
    for g in groups:
        slope, sink = per_head(slopes_ref, g), per_head(sinks_ref, g)
        t_b = s_b[g] - slope * jnp.concatenate([dist] * A_GROUP, axis=1)
        t_m = s_m[g] - slope * jnp.concatenate([dist_m] * A_GROUP, axis=1)
        mx = jnp.maximum(jnp.maximum(jnp.max(t_b, axis=0, keepdims=True),
                                     jnp.max(t_m, axis=0, keepdims=True)), sink)
        e_b.append(jnp.exp(t_b - mx).astype(BF16))
        e_m.append(jnp.exp(t_m - mx).astype(BF16))
        extra.append(jnp.exp(sink - mx))

    def vt_ones(g, rows):
        return jnp.concatenate([vt_ref[0, g * HEAD_DIM:(g + 1) * HEAD_DIM, rows],
                                jnp.ones((A_VROWS - HEAD_DIM, rows.size), BF16)], axis=0)

    acc = [jnp.dot(vt_ones(g, band), e_b[g], preferred_element_type=F32)
           + jnp.dot(vt_ones(g, meta), e_m[g], preferred_element_type=F32) for g in groups]
    for g in groups:
        o_t = acc[g][:HEAD_DIM] / (acc[g][HEAD_DIM:HEAD_DIM + 1] + extra[g])
        for p in range(2):
            pair = jnp.concatenate([o_t[:, (2 * p) * TQ:(2 * p + 1) * TQ],
                                    o_t[:, (2 * p + 1) * TQ:(2 * p + 2) * TQ]], axis=0)
            o_ref[0, :, (2 * g + p) * BLK:(2 * g + p + 1) * BLK] = pair.T.astype(o_ref.dtype)


def _attn_a(proj, vt, slopes, sinks, seq):
    bsz, lp, _ = proj.shape
    kcol = A_Q // A_KV
    return pl.pallas_call(
        functools.partial(_attn_a_kernel, seq=seq),
        grid=(bsz, pl.cdiv(lp, TQ)),
        in_specs=[
            pl.BlockSpec(memory_space=pltpu.SMEM),
            pl.BlockSpec(memory_space=pltpu.SMEM),
            pl.BlockSpec((1, TQ, A_Q), lambda b, i: (b, i, 0)),
            pl.BlockSpec((1, lp, A_KV), lambda b, i: (b, 0, kcol)),
            pl.BlockSpec((1, A_KV, lp), lambda b, i: (b, 0, 0)),
        ],
        out_specs=pl.BlockSpec((1, TQ, A_Q), lambda b, i: (b, i, 0)),
        out_shape=jax.ShapeDtypeStruct((bsz, lp, A_Q), BF16),
        compiler_params=_params(("arbitrary", "arbitrary")),
        name="attn_a",
    )(slopes, sinks, proj, proj, vt)


def _attn_b_kernel(*refs, seq):
    n_part = B_STREAMS // B_PART
    q_refs, k_refs = refs[:n_part], refs[n_part:2 * n_part]
    vt_ref, o_ref, run_ref, acc_ref = refs[2 * n_part:]
    i = pl.program_id(2)
    n_ft = seq // TQ
    w = 2 * TQ

    def part(part_refs, g, rows):
        lo = (g % B_PART) * BLK
        return part_refs[g // B_PART][0, rows, lo:lo + BLK]

    streams = range(B_STREAMS)
    qd = [_split_heads_rows(part(q_refs, g, slice(None)) * SCALE)
          for g in streams]
    run_ref[...] = jnp.zeros_like(run_ref)
    acc_ref[...] = jnp.zeros_like(acc_ref)

    def tri(tk):
        s_idx = lax.broadcasted_iota(jnp.int32, (tk + 16, tk), 0)
        j_idx = lax.broadcasted_iota(jnp.int32, (tk + 16, tk), 1)
        return jnp.where((j_idx >= s_idx) | (s_idx >= tk), 1.0, 0.0).astype(BF16)

    def tile(rows, tri_op, mask):
        zs = [_nt_dot(part(k_refs, g, rows), qd[g]) for g in streams]
        tk = rows.size
        newers = []
        for g in streams:
            sp = jnp.maximum(zs[g], 0.0) + jnp.log(1.0 + jnp.exp(-jnp.abs(zs[g])))
            if mask is not None:
                sp = jnp.where(mask, sp, 0.0)
            newers.append(jnp.dot(tri_op, sp.astype(BF16), preferred_element_type=F32))
        pvs = []
        for g in streams:
            wgt = jnp.exp(zs[g] - (newers[g][:tk] + run_ref[g]))
            if mask is not None:
                wgt = jnp.where(mask, wgt, 0.0)
            pvs.append(jnp.dot(vt_ref[0, g * BLK:(g + 1) * BLK, rows], wgt.astype(BF16),
                               preferred_element_type=F32))
        for g in streams:
            acc_ref[g] += pvs[g]
            run_ref[g] += newers[g][tk:tk + 1]

    @pl.when(i < n_ft)
    def _():
        tri_tq = tri(TQ)
        r = lax.broadcasted_iota(jnp.int32, (TQ, w), 0)
        c = lax.broadcasted_iota(jnp.int32, (TQ, w), 1) & (TQ - 1)
        tile(pl.ds(pl.multiple_of(i * TQ, TQ), TQ), tri_tq, r < c)

        def body(jj, carry):
            tile(pl.ds(pl.multiple_of((i - 1 - jj) * TQ, TQ), TQ), tri_tq, None)
            return carry

        lax.fori_loop(0, i, body, 0)

    r = lax.broadcasted_iota(jnp.int32, (N_META, w), 0)
    c = lax.broadcasted_iota(jnp.int32, (N_META, w), 1) & (TQ - 1)
    tile(pl.ds(seq, N_META), tri(N_META), (i < n_ft) | (r < c))

    for g in streams:
        acc = acc_ref[g]
        o_t = jnp.concatenate([acc[:HEAD_DIM, :TQ], acc[HEAD_DIM:, TQ:]], axis=0)
        o_ref[0, :, g * BLK:(g + 1) * BLK] = o_t.T.astype(o_ref.dtype)


def _attn_b(proj, vt, seq):
    bsz, lp, _ = proj.shape
    gw = B_STREAMS * BLK
    pw = B_PART * BLK
    n_part = B_STREAMS // B_PART
    assert (A_Q + 2 * A_KV) % pw == 0 and B_W % gw == 0 and B_STREAMS % B_PART == 0
    qcol = (A_Q + 2 * A_KV) // pw
    kcol = qcol + B_W // pw

    def col_specs(rows, first, row_index):
        return [pl.BlockSpec((1, rows, pw),
                             lambda b, p, i, s=s: (b, row_index(i), first + p * n_part + s))
                for s in range(n_part)]

    return pl.pallas_call(
        functools.partial(_attn_b_kernel, seq=seq),
        grid=(bsz, B_W // gw, pl.cdiv(lp, TQ)),
        in_specs=col_specs(TQ, qcol, lambda i: i) + col_specs(lp, kcol, lambda i: 0) + [
            pl.BlockSpec((1, gw, lp), lambda b, p, i: (b, p, 0)),
        ],
        out_specs=pl.BlockSpec((1, TQ, gw), lambda b, p, i: (b, i, p)),
        out_shape=jax.ShapeDtypeStruct((bsz, lp, B_W), BF16),
        scratch_shapes=[pltpu.VMEM((B_STREAMS, 1, 2 * TQ), F32),
                        pltpu.VMEM((B_STREAMS, BLK, 2 * TQ), F32)],
        compiler_params=_params(("arbitrary", "arbitrary", "arbitrary")),
        name="attn_b",
    )(*([proj] * (2 * n_part)), vt)


def _attn_c_kernel(slopes_ref, lam_ref, g_ref, q_ref, k_ref, vt_ref, o_ref,
                   m_ref, acc_ref, *, seq, lambda_init):
    p = pl.program_id(1)
    i = pl.program_id(2)
    n_ft = seq // TQ
    w = 2 * TQ
    slope = [slopes_ref[p * NSTREAM + g] for g in range(NSTREAM)]
    qd = [_split_heads_rows(q_ref[0, :, g * BLK:(g + 1) * BLK] * SCALE)
          for g in range(NSTREAM)]
    meta = pl.ds(seq, N_META)

    def cols(g):
        return slice(g * BLK, (g + 1) * BLK)

    def vt_ones(g, rows):
        tk = rows.size
        return jnp.concatenate([vt_ref[0, cols(g), rows], jnp.ones((ACC_ROWS - BLK, tk), BF16)],
                               axis=0)

    def meta_scores(g):
        return _nt_dot(k_ref[0, meta, cols(g)], qd[g])

    rm = lax.broadcasted_iota(jnp.int32, (N_META, w), 0)
    cm = lax.broadcasted_iota(jnp.int32, (N_META, w), 1) & (TQ - 1)

    @pl.when(i == n_ft)
    def _():
        dist = jnp.abs(cm - rm).astype(F32)
        for g in range(NSTREAM):
            t = meta_scores(g) - slope[g] * dist
            m0 = jnp.max(t, axis=0, keepdims=True)
            e0 = jnp.exp(t - m0)
            m_ref[g] = m0
            acc_ref[g] = jnp.dot(vt_ones(g, meta), e0.astype(BF16), preferred_element_type=F32)

    @pl.when(i < n_ft)
    def _():
        r = lax.broadcasted_iota(jnp.int32, (TQ, w), 0)
        c = lax.broadcasted_iota(jnp.int32, (TQ, w), 1) & (TQ - 1)
        lane_k = lax.broadcasted_iota(jnp.int32, (TQ, BLK), 1)
        row_k = lax.broadcasted_iota(jnp.int32, (TQ, BLK), 0)
        key_off = jnp.where(lane_k < 3, row_k, 0).astype(F32).astype(BF16)
        lane_q = lax.broadcasted_iota(jnp.int32, (16, BLK), 1)
        qa = []
        for g in range(NSTREAM):
            sl = jnp.full((16, BLK), slope[g], F32)
            s_hi = sl.astype(BF16).astype(F32)
            s_mid = (sl - s_hi).astype(BF16).astype(F32)
            s_lo = (sl - s_hi) - s_mid
            sl3 = jnp.where(lane_q == 0, s_hi,
                            jnp.where(lane_q == 1, s_mid, jnp.where(lane_q == 2, s_lo, 0.0)))
            sl3 = jnp.concatenate([sl3.astype(BF16)] * (w // 16), axis=0)
            qa.append(jnp.concatenate([qd[g], sl3], axis=1))

        def scores(g, rows):
            return _nt_dot(jnp.concatenate([k_ref[0, rows, cols(g)], key_off], axis=1), qa[g])

        diag = pl.ds(pl.multiple_of(i * TQ, TQ), TQ)
        vis = (r >> CHUNK_SHIFT) <= (c >> CHUNK_SHIFT)
        ahead = jnp.where(vis, 2.0 * jnp.maximum(r - c, 0).astype(F32), HIDE)
        meta_off = (rm - (N_META + i * TQ)).astype(F32)
        s_d = [scores(g, diag) for g in range(NSTREAM)]
        s_m = [meta_scores(g) for g in range(NSTREAM)]
        e_d, e_m = [], []
        for g in range(NSTREAM):
            t_d = s_d[g] - slope[g] * ahead
            t_m = s_m[g] + slope[g] * meta_off
            m0 = jnp.maximum(jnp.max(t_d, axis=0, keepdims=True),
                             jnp.max(t_m, axis=0, keepdims=True))
            e_d.append(jnp.exp(t_d - m0).astype(BF16))
            e_m.append(jnp.exp(t_m - m0).astype(BF16))
            m_ref[g] = m0
        for g in range(NSTREAM):
            acc_ref[g] = (jnp.dot(vt_ones(g, diag), e_d[g], preferred_element_type=F32)
                          + jnp.dot(vt_ones(g, meta), e_m[g], preferred_element_type=F32))

        def older_tile(j):
            rows = pl.ds(pl.multiple_of(j * TQ, TQ), TQ)
            s = [scores(g, rows) for g in range(NSTREAM)]
            es, corrs = [], []
            for g in range(NSTREAM):
                cj = slope[g] * ((i - j) * TQ).astype(F32)
                m_old = m_ref[g]
                m_new = jnp.maximum(m_old, jnp.max(s[g], axis=0, keepdims=True) - cj)
                es.append(jnp.exp(s[g] - (m_new + cj)).astype(BF16))
                corrs.append(jnp.exp(m_old - m_new))
                m_ref[g] = m_new
            pvs = [jnp.dot(vt_ones(g, rows), es[g], preferred_element_type=F32)
                   for g in range(NSTREAM)]
            for g in range(NSTREAM):
                acc_ref[g] = corrs[g] * acc_ref[g] + pvs[g]

        def body(j, carry):
            older_tile(j)
            return carry

        lax.fori_loop(0, i, body, 0)

    lv = lam_ref[...]
    t1 = jnp.sum(lv[0:1] * lv[1:2], axis=-1, keepdims=True)
    t2 = jnp.sum(lv[2:3] * lv[3:4], axis=-1, keepdims=True)
    lam = jnp.exp(t1) - jnp.exp(t2) + lambda_init

    for g in range(NSTREAM):
        acc = acc_ref[g]
        o_all = acc[:BLK] / acc[BLK:BLK + 1]
        o_t = o_all[:, :TQ] - lam * o_all[:, TQ:]
        ms = jnp.mean(o_t * o_t, axis=0, keepdims=True)
        o = (o_t * lax.rsqrt(ms + EPS)).T * g_ref[...] * (1.0 - lambda_init)
        o_ref[0, :, cols(g)] = o.astype(o_ref.dtype)


def _attn_c(proj, vt, slopes, lam_vecs, subln_g, lambda_init, seq):
    bsz, lp, _ = proj.shape
    gw = NSTREAM * BLK
    kcol = C_QK // gw
    return pl.pallas_call(
        functools.partial(_attn_c_kernel, seq=seq, lambda_init=lambda_init),
        grid=(bsz, C_QK // gw, pl.cdiv(lp, TQ)),
        in_specs=[
            pl.BlockSpec(memory_space=pltpu.SMEM),
            pl.BlockSpec((4, HEAD_DIM), lambda b, p, i: (0, 0)),
            pl.BlockSpec((1, 2 * HEAD_DIM), lambda b, p, i: (0, 0)),
            pl.BlockSpec((1, TQ, gw), lambda b, p, i: (b, i, p)),
            pl.BlockSpec((1, lp, gw), lambda b, p, i: (b, 0, kcol + p)),
            pl.BlockSpec((1, gw, lp), lambda b, p, i: (b, p, 0)),
        ],
        out_specs=pl.BlockSpec((1, TQ, gw), lambda b, p, i: (b, i, p)),
        out_shape=jax.ShapeDtypeStruct((bsz, lp, C_QK), BF16),
        scratch_shapes=[pltpu.VMEM((NSTREAM, 1, 2 * TQ), F32),
                        pltpu.VMEM((NSTREAM, ACC_ROWS, 2 * TQ), F32)],
        compiler_params=_params(("arbitrary", "arbitrary", "arbitrary")),
        name="attn_c",
    )(slopes, lam_vecs, subln_g.reshape(1, 2 * HEAD_DIM), proj, proj, vt)


def _alibi_slopes(n):
    return jnp.exp2(-8.0 * (jnp.arange(n, dtype=F32) + 1.0) / n)


def kernel(x, meta_tokens, ab_norm, w_in_ab, attn_sinks, w_out_ab, c_norm, w_in_c,
           diff_lambda, diff_subln, w_out_c, mlp_norm, w_mlp_in, w_mlp_out, final_norm):
    bsz, seq, d = x.shape
    lp = seq + BLK
    assert seq % TQ == 0 and d == D_MODEL
    t = bsz * lp
    depth = mlp_norm.shape[0]

    tail = jnp.concatenate([meta_tokens.astype(x.dtype), jnp.zeros((BLK - N_META, d), x.dtype)])
    h = (x, tail)

    tm = lp // 2

    for layer in range(depth):
        li = layer // 2
        if layer % 2 == 0:
            proj = _norm_matmul(h, ab_norm[li], w_in_ab, li, relu2=False, tm=tm, tn=768)
            proj = proj.reshape(bsz, lp, AB_IN)
            vt_a = _transposed_cols(proj, A_Q + A_KV, A_KV)
            out_a = _attn_a(proj, vt_a, _alibi_slopes(A_HEADS), attn_sinks[li].astype(F32), seq)
            vt_b = _transposed_cols(proj, AB_IN - B_W, B_W)
            out_b = _attn_b(proj, vt_b, seq)
            h = _matmul_res([out_a.reshape(t, A_Q), out_b.reshape(t, B_W)], w_out_ab, li, h,
                            tm=lp, tn=512, tk=A_Q + B_W)
        else:
            lambda_init = 0.8 - 0.6 * math.exp(-0.3 * layer)
            proj = _norm_matmul(h, c_norm[li], w_in_c, li, relu2=False, tm=tm, tn=1024)
            proj = proj.reshape(bsz, lp, 3 * C_QK)
            vt_c = _transposed_cols(proj, 2 * C_QK, C_QK)
            out_c = _attn_c(proj, vt_c, _alibi_slopes(C_HEADS), diff_lambda[li].astype(F32),
                            diff_subln[li].astype(F32), lambda_init, seq)
            h = _matmul_res([out_c.reshape(t, C_QK)], w_out_c, li, h, tm=lp, tn=512, tk=2048)
        u = _norm_matmul(h, mlp_norm[layer], w_mlp_in, layer, relu2=True, tm=tm, tn=1024)
        h = _matmul_res([u], w_mlp_out, layer, h, tm=tm, tn=1024, tk=2048)

    return _final_norm(h.reshape(bsz, lp, d), final_norm, seq)
```
